```python
import math
import functools
import jax
import jax.numpy as jnp
from jax import lax
import numpy as np

D_MODEL = 1024
BATCH = 4
SEQ = 8192
DEPTH = 2
DEC_BATCH = 128
DEC_SEQ = 8
PAST_LEN = 16384
PAGE_SIZE = 128

H_A = 8
DH_A = 64
A_W = H_A * DH_A
S5_GROUPS = 32
S5_P = 16
S5_N = 64
B_W = S5_GROUPS * S5_P
IN_E = 3 * A_W + H_A + B_W
OUT_E = A_W + B_W
H_C = 8
DH_C = 64
C_W = H_C * DH_C
W_LORA = 32
A_LORA = 32
G_LORA = 96
C_IN = 3 * C_W + W_LORA + A_LORA + G_LORA
RWKV_GN_EPS = 64e-5
H_D = 8
Q_LORA = 256
KV_LORA = 128
NOPE_D = 64
ROPE_D = 32
V_D = 64
ROPE_THETA = 10000.0
D_IN = Q_LORA + KV_LORA + ROPE_D
IN_O = C_IN + D_IN
OUT_O = C_W + H_D * V_D
D_FF = 2816
CONV_W = 3
PLE_DIM = 256
Q_BLOCK = 128
NORM_EPS = 1e-6
POOL_NUM = 5
POOL_DEN = 4
NEG = -1e30

kernel_name = 'hybrid_fox_s5_rwkv7_mla_step'


def rmsnorm(x, g):
    xf = x.astype(jnp.float32)
    y = xf * lax.rsqrt(jnp.mean(xf * xf, axis=-1, keepdims=True) + NORM_EPS)
    return (y * g.astype(jnp.float32)).astype(x.dtype)


def rope(x, pos):
    half = ROPE_D // 2
    inv = ROPE_THETA ** (-jnp.arange(half, dtype=jnp.float32) / half)
    ang = pos.astype(jnp.float32)[:, None] * inv[None, :]
    shape = (ang.shape[0],) + (1,) * (x.ndim - 3) + (half,)
    cos = jnp.cos(ang).reshape(shape)
    sin = jnp.sin(ang).reshape(shape)
    x1 = x[..., :half].astype(jnp.float32)
    x2 = x[..., half:].astype(jnp.float32)
    return jnp.concatenate([x1 * cos - x2 * sin, x2 * cos + x1 * sin], axis=-1).astype(x.dtype)


def suffix_sum_after(lf):
    rc = lax.cumsum(lf, axis=1, reverse=True)
    return jnp.concatenate([rc[:, 1:], jnp.zeros_like(rc[:, :1])], axis=1)


def fox_attend(q, k, v, suf_q, suf_k, q_pos, k_pos):
    s = jnp.einsum('bqhd,bkhd->bhqk', q, k).astype(jnp.float32) * (DH_A ** -0.5)
    s = s + jnp.swapaxes(suf_k, 1, 2)[:, :, None, :] - jnp.swapaxes(suf_q, 1, 2)[:, :, :, None]
    s = jnp.where(k_pos[None, :] <= q_pos[:, None], s, NEG)
    p = jax.nn.softmax(s, axis=-1)
    return jnp.einsum('bhqk,bkhd->bqhd', p.astype(v.dtype), v)


def fox_prompt(j, q, k, v, lf):
    S = q.shape[1]
    pos = jnp.arange(S)

    def block(i):
        q0 = i * Q_BLOCK
        lfm = jnp.where((pos <= q0 + Q_BLOCK - 1)[None, :, None], lf, 0.0)
        suf = suffix_sum_after(lfm)
        qb = lax.dynamic_slice_in_dim(q, q0, Q_BLOCK, axis=1)
        sq = lax.dynamic_slice_in_dim(suf, q0, Q_BLOCK, axis=1)
        return fox_attend(qb, k, v, sq, suf, q0 + jnp.arange(Q_BLOCK), pos)

    out = lax.map(block, jnp.arange(S // Q_BLOCK))
    return jnp.moveaxis(out, 0, 1).reshape(q.shape)


def fox_sample(j, q, k, v, lf, cache_k, cache_v, cache_lf, page_table):
    T = q.shape[1]
    past = page_table.shape[1] * cache_k.shape[2]
    q_pos = past + jnp.arange(T)
    k_pos = jnp.arange(past + T)

    def one(args):
        qb, kn, vn, lfn, pages = args
        kk = jnp.concatenate([cache_k[j, pages].reshape(past, H_A, DH_A), kn.astype(cache_k.dtype)], axis=0)
        vv = jnp.concatenate([cache_v[j, pages].reshape(past, H_A, DH_A), vn.astype(cache_v.dtype)], axis=0)
        ll = jnp.concatenate([cache_lf[j, pages].reshape(past, H_A).astype(jnp.float32), lfn], axis=0)
        suf = suffix_sum_after(ll[None])
        return fox_attend(qb[None], kk[None], vv[None], suf[:, past:], suf, q_pos, k_pos)[0]

    return lax.map(one, (q, k, v, lf, page_table))


def mla_attend(q_abs, q_pe, c, kpe, q_pos, k_pos):
    s = (jnp.einsum('bqhc,bkc->bhqk', q_abs, c) + jnp.einsum('bqhr,bkr->bhqk', q_pe, kpe)).astype(jnp.float32)
    s = s * ((NOPE_D + ROPE_D) ** -0.5)
    s = jnp.where(k_pos[None, :] <= q_pos[:, None], s, NEG)
    p = jax.nn.softmax(s, axis=-1)
    return jnp.einsum('bhqk,bkc->bqhc', p.astype(c.dtype), c)


def mla_prompt(j, q_abs, q_pe, c, kpe):
    S = q_abs.shape[1]
    pos = jnp.arange(S)

    def block(i):
        q0 = i * Q_BLOCK
        qa = lax.dynamic_slice_in_dim(q_abs, q0, Q_BLOCK, axis=1)
        qp = lax.dynamic_slice_in_dim(q_pe, q0, Q_BLOCK, axis=1)
        return mla_attend(qa, qp, c, kpe, q0 + jnp.arange(Q_BLOCK), pos)

    out = lax.map(block, jnp.arange(S // Q_BLOCK))
    return jnp.moveaxis(out, 0, 1).reshape(q_abs.shape)


def mla_sample(j, q_abs, q_pe, c, kpe, cache_ckv, cache_kpe, page_table):
    T = q_abs.shape[1]
    past = page_table.shape[1] * cache_ckv.shape[2]
    q_pos = past + jnp.arange(T)
    k_pos = jnp.arange(past + T)

    def one(args):
        qa, qp, cn, kn, pages = args
        cc = jnp.concatenate([cache_ckv[j, pages].reshape(past, KV_LORA), cn.astype(cache_ckv.dtype)], axis=0)
        kk = jnp.concatenate([cache_kpe[j, pages].reshape(past, ROPE_D), kn.astype(cache_kpe.dtype)], axis=0)
        return mla_attend(qa[None], qp[None], cc[None], kk[None], q_pos, k_pos)[0]

    return lax.map(one, (q_abs, q_pe, c, kpe, page_table))


def s5_mix(u, h_re, h_im, W, j):
    f32 = jnp.float32
    Bt, T, _ = u.shape
    ug = u.astype(f32).reshape(Bt, T, S5_GROUPS, S5_P)
    lr = W['s5_lam_re'][j].astype(f32)
    li = W['s5_lam_im'][j].astype(f32)
    dt = jnp.exp(W['s5_log_dt'][j].astype(f32))[:, None]
    mag = jnp.exp(lr * dt)
    ang = li * dt
    ab_re = mag * jnp.cos(ang)
    ab_im = mag * jnp.sin(ang)
    den = lr * lr + li * li
    f_re = ((ab_re - 1.0) * lr + ab_im * li) / den
    f_im = (ab_im * lr - (ab_re - 1.0) * li) / den
    br = W['s5_b_re'][j].astype(f32)
    bi = W['s5_b_im'][j].astype(f32)
    bb_re = f_re[..., None] * br - f_im[..., None] * bi
    bb_im = f_re[..., None] * bi + f_im[..., None] * br
    bu_re = jnp.einsum('gnp,btgp->btgn', bb_re, ug)
    bu_im = jnp.einsum('gnp,btgp->btgn', bb_im, ug)
    a_re = jnp.broadcast_to(ab_re, bu_re.shape)
    a_im = jnp.broadcast_to(ab_im, bu_im.shape)

    def comb(x, y):
        a1r, a1i, b1r, b1i = x
        a2r, a2i, b2r, b2i = y
        return (a2r * a1r - a2i * a1i, a2r * a1i + a2i * a1r,
                a2r * b1r - a2i * b1i + b2r, a2r * b1i + a2i * b1r + b2i)

    A_r, A_i, B_r, B_i = lax.associative_scan(comb, (a_re, a_im, bu_re, bu_im), axis=1)
    h_re = h_re.astype(f32)[:, None]
    h_im = h_im.astype(f32)[:, None]
    x_re = A_r * h_re - A_i * h_im + B_r
    x_im = A_r * h_im + A_i * h_re + B_i
    y = (jnp.einsum('gpn,btgn->btgp', W['s5_c_re'][j].astype(f32), x_re)
         - jnp.einsum('gpn,btgn->btgp', W['s5_c_im'][j].astype(f32), x_im)
         + W['s5_d'][j].astype(f32).reshape(S5_GROUPS, S5_P) * ug)
    return y.reshape(Bt, T, B_W), x_re[:, -1], x_im[:, -1]


def rwkv_mix(z, shift0, S0, W, j):
    f32 = jnp.float32
    Bt, T, _ = z.shape
    zf = z.astype(f32)
    zprev = jnp.concatenate([shift0[:, None].astype(f32), zf[:, :-1]], axis=1)
    zm = zf + (zprev - zf) * W['rw_mu'][j]
    r = zm[..., :C_W]
    k = zm[..., C_W:2 * C_W]
    v = zm[..., 2 * C_W:3 * C_W]
    o = 3 * C_W
    wd = zm[..., o:o + W_LORA]
    ad = zm[..., o + W_LORA:o + W_LORA + A_LORA]
    gd = zm[..., o + W_LORA + A_LORA:]
    w_log = -jax.nn.softplus(-(W['rw_w0'][j] + jnp.tanh(wd) @ W['rw_w2'][j])) - 0.5
    decay = jnp.exp(-jnp.exp(w_log))
    a = jax.nn.sigmoid(W['rw_a0'][j] + ad @ W['rw_a2'][j])
    g = jax.nn.sigmoid(gd) @ W['rw_g2'][j]

    def heads(t):
        return t.reshape(Bt, T, H_C, DH_C)

    kk = heads(k * W['rw_kk'][j])
    kk = kk / jnp.maximum(jnp.linalg.norm(kk, axis=-1, keepdims=True), 1e-12)
    k = k * (1.0 + (a - 1.0) * W['rw_ka'][j])
    rh, kh, vh, wh, ah = heads(r), heads(k), heads(v), heads(decay), heads(a)

    def step(S, xs):
        r_t, w_t, k_t, v_t, kk_t, a_t = xs
        sk = jnp.einsum('bhvk,bhk->bhv', S, kk_t)
        S = (S * w_t[:, :, None, :] - sk[..., None] * (kk_t * a_t)[:, :, None, :]
             + v_t[..., None] * k_t[:, :, None, :])
        return S, jnp.einsum('bhvk,bhk->bhv', S, r_t)

    xs = tuple(jnp.moveaxis(t, 1, 0) for t in (rh, wh, kh, vh, kk, ah))
    S, ys = lax.scan(step, S0.astype(f32), xs)
    y = jnp.moveaxis(ys, 0, 1)
    mu = jnp.mean(y, axis=-1, keepdims=True)
    var = jnp.mean(jnp.square(y - mu), axis=-1, keepdims=True)
    y = ((y - mu) * lax.rsqrt(var + RWKV_GN_EPS)).reshape(Bt, T, C_W) * W['rw_lnw'][j] + W['rw_lnb'][j]
    bonus = jnp.sum(rh * kh * W['rw_rk'][j], axis=-1, keepdims=True) * vh
    out = (y + bonus.reshape(Bt, T, C_W)) * g
    return out.astype(z.dtype), S, z[:, -1]


def even_mixer(x, s5_re, s5_im, fox_fn, W, j):
    Bt, T, _ = x.shape
    z = x @ W['w_in_e'][j]
    q = z[..., :A_W].reshape(Bt, T, H_A, DH_A)
    k = z[..., A_W:2 * A_W].reshape(Bt, T, H_A, DH_A)
    v = z[..., 2 * A_W:3 * A_W].reshape(Bt, T, H_A, DH_A)
    lf = jax.nn.log_sigmoid(z[..., 3 * A_W:3 * A_W + H_A].astype(jnp.float32) + W['b_f'][j])
    u = z[..., 3 * A_W + H_A:]
    o_a = fox_fn(q, k, v, lf).reshape(Bt, T, A_W)
    y_b, n_re, n_im = s5_mix(u, s5_re, s5_im, W, j)
    y_b = jax.nn.gelu(y_b)
    y_b = y_b * jax.nn.sigmoid(y_b @ W['w_glu'][j] + W['b_glu'][j])
    o = jnp.concatenate([o_a, y_b.astype(o_a.dtype)], axis=-1) @ W['w_out_e'][j]
    return o, (k, v, lf), (n_re, n_im)


def odd_mixer(x, pos, rw_S, rw_sh, mla_fn, W, j):
    Bt, T, _ = x.shape
    z = x @ W['w_in_o'][j]
    o_c, S, sh = rwkv_mix(z[..., :C_IN], rw_sh, rw_S, W, j)
    zd = z[..., C_IN:]
    q_lat = zd[..., :Q_LORA]
    kv_lat = zd[..., Q_LORA:Q_LORA + KV_LORA]
    kpe_raw = zd[..., Q_LORA + KV_LORA:]
    qf = jnp.einsum('btc,chd->bthd', rmsnorm(q_lat, W['mla_q_norm'][j]), W['w_q_up'][j])
    q_nope = qf[..., :NOPE_D]
    q_pe = rope(qf[..., NOPE_D:], pos)
    c = rmsnorm(kv_lat, W['mla_kv_norm'][j])
    kpe = rope(kpe_raw, pos)
    w_kv = W['w_kv_up'][j]
    q_abs = jnp.einsum('bthd,chd->bthc', q_nope, w_kv[..., :NOPE_D])
    lat = mla_fn(q_abs, q_pe, c, kpe)
    o_d = jnp.einsum('bthc,chv->bthv', lat, w_kv[..., NOPE_D:]).reshape(Bt, T, H_D * V_D)
    o = jnp.concatenate([o_c, o_d.astype(o_c.dtype)], axis=-1) @ W['w_out_o'][j]
    return o, (S, sh), (c, kpe)


def conv_ffn(x, buf, w_up, conv_w, conv_b, w_down):
    T = x.shape[1]
    up = x @ w_up
    a = up[..., :D_FF]
    b = up[..., D_FF:]
    ext = jnp.concatenate([buf.astype(a.dtype), a], axis=1)
    c = conv_b
    for tap in range(CONV_W):
        c = c + conv_w[tap] * ext[:, tap:tap + T]
    return (jax.nn.gelu(c) * b) @ w_down, ext[:, T:]


def per_layer_embed(h, p, w_ple, norm_ple, w_pg, b_pg):
    gate = jax.nn.sigmoid(rmsnorm(h, norm_ple) @ w_pg + b_pg)
    return h + gate * (p @ w_ple)


def run_trunk(h, p, pos, fox_fn, mla_fn, s5_re, s5_im, rw_state, rw_shift, ffn_buf, W):
    fox_rows, s5_states, rw_states, mla_rows, ffn_bufs = [], [], [], [], []
    for i in range(DEPTH):
        j = i // 2
        x = rmsnorm(h, W['norm_mix'][i])
        if i % 2 == 0:
            o, rows, st = even_mixer(x, s5_re[j], s5_im[j], functools.partial(fox_fn, j), W, j)
            fox_rows.append(rows)
            s5_states.append(st)
        else:
            o, st, rows = odd_mixer(x, pos, rw_state[j], rw_shift[j], functools.partial(mla_fn, j), W, j)
            rw_states.append(st)
            mla_rows.append(rows)
        h = h + o
        f, nb = conv_ffn(rmsnorm(h, W['norm_ffn'][i]), ffn_buf[i], W['w_ffn_up'][i],
                         W['ffn_conv_w'][i], W['ffn_conv_b'][i], W['w_ffn_down'][i])
        ffn_bufs.append(nb)
        h = h + f
        h = per_layer_embed(h, p[i], W['w_ple'][i], W['norm_ple'][i], W['w_pg'][i], W['b_pg'][i])
    y = rmsnorm(h, W['norm_f'])

    def stk(xs, n):
        return jnp.stack([r[n] for r in xs])

    return (y, stk(fox_rows, 0), stk(fox_rows, 1), stk(fox_rows, 2), stk(s5_states, 0), stk(s5_states, 1),
            stk(rw_states, 0), stk(rw_states, 1), stk(mla_rows, 0), stk(mla_rows, 1), jnp.stack(ffn_bufs))


def setup_inputs(seed: int = 0) -> dict:
    key = jax.random.key(seed)
    keys = iter(jax.random.split(key, 96))

    def nrm(shape, scale=1.0):
        return jax.random.normal(next(keys), shape, jnp.float32) * scale

    def unif(shape, lo, hi):
        return jax.random.uniform(next(keys), shape, jnp.float32, lo, hi)

    def gain(shape):
        return 1.0 + 0.02 * nrm(shape)

    ne = (DEPTH + 1) // 2
    no = DEPTH // 2
    n_pages = PAST_LEN // PAGE_SIZE
    n_pool = (DEC_BATCH * n_pages * POOL_NUM) // POOL_DEN
    page_table = jax.random.permutation(next(keys), n_pool)[:DEC_BATCH * n_pages]
    page_table = page_table.reshape(DEC_BATCH, n_pages).astype(jnp.int32)
    cache_fox_lf = jax.nn.log_sigmoid(unif((ne, 1, 1, H_A), 1.0, 6.0) + 0.5 * nrm((ne, n_pool, PAGE_SIZE, H_A)))
    return {
        'x_prompt': nrm((BATCH, SEQ, D_MODEL)),
        'x_sample': nrm((DEC_BATCH, DEC_SEQ, D_MODEL)),
        'cache_fox_k': nrm((ne, n_pool, PAGE_SIZE, H_A, DH_A)),
        'cache_fox_v': nrm((ne, n_pool, PAGE_SIZE, H_A, DH_A)),
        'cache_fox_lf': cache_fox_lf,
        'state_s5_re': nrm((ne, DEC_BATCH, S5_GROUPS, S5_N), 0.1),
        'state_s5_im': nrm((ne, DEC_BATCH, S5_GROUPS, S5_N), 0.1),
        'state_rwkv': nrm((no, DEC_BATCH, H_C, DH_C, DH_C), 0.3),
        'state_shift': nrm((no, DEC_BATCH, C_IN)),
        'cache_mla_ckv': nrm((no, n_pool, PAGE_SIZE, KV_LORA)),
        'cache_mla_kpe': nrm((no, n_pool, PAGE_SIZE, ROPE_D)),
        'state_ffn_conv': nrm((DEPTH, DEC_BATCH, CONV_W - 1, D_FF)),
        'page_table': page_table,
        'p_prompt': nrm((DEPTH, BATCH, SEQ, PLE_DIM)),
        'p_sample': nrm((DEPTH, DEC_BATCH, DEC_SEQ, PLE_DIM)),
        'norm_mix': gain((DEPTH, D_MODEL)),
        'w_in_e': nrm((ne, D_MODEL, IN_E), D_MODEL ** -0.5),
        'b_f': unif((ne, H_A), 1.0, 6.0),
        's5_lam_re': -0.5 + 0.01 * nrm((ne, S5_GROUPS, S5_N)),
        's5_lam_im': jnp.pi * jnp.arange(S5_N, dtype=jnp.float32) + 0.01 * nrm((ne, S5_GROUPS, S5_N)),
        's5_log_dt': unif((ne, S5_GROUPS), math.log(1e-3), math.log(1e-1)),
        's5_b_re': nrm((ne, S5_GROUPS, S5_N, S5_P), (2 * S5_P) ** -0.5),
        's5_b_im': nrm((ne, S5_GROUPS, S5_N, S5_P), (2 * S5_P) ** -0.5),
        's5_c_re': nrm((ne, S5_GROUPS, S5_P, S5_N), S5_N ** -0.5),
        's5_c_im': nrm((ne, S5_GROUPS, S5_P, S5_N), S5_N ** -0.5),
        's5_d': nrm((ne, B_W)),
        'w_glu': nrm((ne, B_W, B_W), B_W ** -0.5),
        'b_glu': nrm((ne, B_W), 0.01),
        'w_out_e': nrm((ne, OUT_E, D_MODEL), 0.5 * OUT_E ** -0.5),
        'w_in_o': nrm((no, D_MODEL, IN_O), D_MODEL ** -0.5),
        'rw_mu': unif((no, C_IN), 0.0, 1.0),
        'rw_w0': unif((no, C_W), -6.5, -1.5),
        'rw_w2': nrm((no, W_LORA, C_W), 0.5 * W_LORA ** -0.5),
        'rw_a0': nrm((no, C_W), 0.1),
        'rw_a2': nrm((no, A_LORA, C_W), 0.5 * A_LORA ** -0.5),
        'rw_g2': nrm((no, G_LORA, C_W), G_LORA ** -0.5),
        'rw_kk': 0.85 + 0.02 * nrm((no, C_W)),
        'rw_ka': gain((no, C_W)),
        'rw_rk': nrm((no, H_C, DH_C), 0.1),
        'rw_lnw': gain((no, C_W)),
        'rw_lnb': nrm((no, C_W), 0.01),
        'mla_q_norm': gain((no, Q_LORA)),
        'w_q_up': nrm((no, Q_LORA, H_D, NOPE_D + ROPE_D), Q_LORA ** -0.5),
        'mla_kv_norm': gain((no, KV_LORA)),
        'w_kv_up': nrm((no, KV_LORA, H_D, NOPE_D + V_D), KV_LORA ** -0.5),
        'w_out_o': nrm((no, OUT_O, D_MODEL), 0.5 * OUT_O ** -0.5),
        'norm_ffn': gain((DEPTH, D_MODEL)),
        'w_ffn_up': nrm((DEPTH, D_MODEL, 2 * D_FF), D_MODEL ** -0.5),
        'ffn_conv_w': nrm((DEPTH, CONV_W, D_FF), CONV_W ** -0.5),
        'ffn_conv_b': nrm((DEPTH, D_FF), 0.01),
        'w_ffn_down': nrm((DEPTH, D_FF, D_MODEL), 0.5 * D_FF ** -0.5),
        'norm_ple': gain((DEPTH, D_MODEL)),
        'w_pg': nrm((DEPTH, D_MODEL, D_MODEL), D_MODEL ** -0.5),
        'b_pg': nrm((DEPTH, D_MODEL), 0.01),
        'w_ple': nrm((DEPTH, PLE_DIM, D_MODEL), PLE_DIM ** -0.5),
        'norm_f': gain((D_MODEL,)),
    }


def reference(x_prompt, x_sample, cache_fox_k, cache_fox_v, cache_fox_lf, state_s5_re, state_s5_im,
              state_rwkv, state_shift, cache_mla_ckv, cache_mla_kpe, state_ffn_conv, page_table,
              p_prompt, p_sample, norm_mix, w_in_e, b_f, s5_lam_re, s5_lam_im, s5_log_dt, s5_b_re, s5_b_im,
              s5_c_re, s5_c_im, s5_d, w_glu, b_glu, w_out_e, w_in_o, rw_mu, rw_w0, rw_w2, rw_a0, rw_a2,
              rw_g2, rw_kk, rw_ka, rw_rk, rw_lnw, rw_lnb, mla_q_norm, w_q_up, mla_kv_norm, w_kv_up, w_out_o,
              norm_ffn, w_ffn_up, ffn_conv_w, ffn_conv_b, w_ffn_down, norm_ple, w_pg, b_pg, w_ple, norm_f):
    W = dict(norm_mix=norm_mix, w_in_e=w_in_e, b_f=b_f, s5_lam_re=s5_lam_re, s5_lam_im=s5_lam_im,
             s5_log_dt=s5_log_dt, s5_b_re=s5_b_re, s5_b_im=s5_b_im, s5_c_re=s5_c_re, s5_c_im=s5_c_im,
             s5_d=s5_d, w_glu=w_glu, b_glu=b_glu, w_out_e=w_out_e, w_in_o=w_in_o, rw_mu=rw_mu, rw_w0=rw_w0,
             rw_w2=rw_w2, rw_a0=rw_a0, rw_a2=rw_a2, rw_g2=rw_g2, rw_kk=rw_kk, rw_ka=rw_ka, rw_rk=rw_rk,
             rw_lnw=rw_lnw, rw_lnb=rw_lnb, mla_q_norm=mla_q_norm, w_q_up=w_q_up, mla_kv_norm=mla_kv_norm,
             w_kv_up=w_kv_up, w_out_o=w_out_o, norm_ffn=norm_ffn, w_ffn_up=w_ffn_up, ffn_conv_w=ffn_conv_w,
             ffn_conv_b=ffn_conv_b, w_ffn_down=w_ffn_down, norm_ple=norm_ple, w_pg=w_pg, b_pg=b_pg,
             w_ple=w_ple, norm_f=norm_f)
    ne = (DEPTH + 1) // 2
    no = DEPTH // 2

    bp = x_prompt.shape[0]
    (y_prompt, fox_k_p, fox_v_p, fox_lf_p, s5_re_p, s5_im_p, rwkv_p, shift_p, ckv_p, kpe_p, ffn_p) = run_trunk(
        x_prompt, p_prompt, jnp.arange(x_prompt.shape[1]), fox_prompt, mla_prompt,
        jnp.zeros((ne, bp, S5_GROUPS, S5_N), jnp.float32), jnp.zeros((ne, bp, S5_GROUPS, S5_N), jnp.float32),
        jnp.zeros((no, bp, H_C, DH_C, DH_C), jnp.float32), jnp.zeros((no, bp, C_IN), x_prompt.dtype),
        jnp.zeros((DEPTH, bp, CONV_W - 1, D_FF), x_prompt.dtype), W)

    past_len = page_table.shape[1] * cache_fox_k.shape[2]
    fox_fn = functools.partial(fox_sample, cache_k=cache_fox_k, cache_v=cache_fox_v, cache_lf=cache_fox_lf,
                               page_table=page_table)
    mla_fn = functools.partial(mla_sample, cache_ckv=cache_mla_ckv, cache_kpe=cache_mla_kpe, page_table=page_table)
    (y_sample, fox_k_s, fox_v_s, fox_lf_s, s5_re_s, s5_im_s, rwkv_s, shift_s, ckv_s, kpe_s, ffn_s) = run_trunk(
        x_sample, p_sample, past_len + jnp.arange(x_sample.shape[1]), fox_fn, mla_fn,
        state_s5_re, state_s5_im, state_rwkv, state_shift, state_ffn_conv, W)

    return (y_prompt, y_sample, fox_k_p, fox_v_p, fox_lf_p, fox_k_s, fox_v_s, fox_lf_s,
            s5_re_p, s5_im_p, s5_re_s, s5_im_s, rwkv_p, shift_p, rwkv_s, shift_s,
            ckv_p, kpe_p, ckv_s, kpe_s, ffn_p, ffn_s)
```

```python
import functools
import math

import jax
import jax.numpy as jnp
from jax import lax
from jax.experimental import pallas as pl
from jax.experimental.pallas import tpu as pltpu

F32 = jnp.float32
BF16 = jnp.bfloat16

D_MODEL = 1024
H_A = 8
DH_A = 64
A_W = H_A * DH_A
S5_GROUPS = 32
S5_P = 16
S5_N = 64
B_W = S5_GROUPS * S5_P
S5_W = S5_GROUPS * S5_N
H_C = 8
DH_C = 64
C_W = H_C * DH_C
W_LORA = 32
A_LORA = 32
G_LORA = 96
C_IN = 3 * C_W + W_LORA + A_LORA + G_LORA
RWKV_GN_EPS = 64e-5
H_D = 8
Q_LORA = 256
KV_LORA = 128
NOPE_D = 64
ROPE_D = 32
V_D = 64
ROPE_THETA = 10000.0
D_FF = 2816
CONV_W = 3
PLE_DIM = 256
NORM_EPS = 1e-6
NEG = -1e30

LANE = 128
SUBLANE = 8
VMEM_LIMIT = 56 * 1024 * 1024
C_PAD = 3 * C_W + 3 * LANE
D_PAD = 512
QK_PAD = 256
PAGES_PER_STEP = 8
S5_SCAN_LANES = 512


def _cp(*sem):
    return pltpu.CompilerParams(dimension_semantics=sem, vmem_limit_bytes=VMEM_LIMIT)


def _tile(n, pref):
    if n <= pref:
        return n
    t = pref - pref % SUBLANE
    while t >= SUBLANE:
        if n % t == 0:
            return t
        t -= SUBLANE
    return n


def _const_spec(shape):
    nd = len(shape)
    return pl.BlockSpec(shape, lambda *_: (0,) * nd)


def _rms(x, g):
    return x * lax.rsqrt(jnp.mean(x * x, axis=-1, keepdims=True) + NORM_EPS) * g


def _split3(x):
    hi = x.astype(BF16)
    r1 = x - hi.astype(F32)
    mid = r1.astype(BF16)
    lo = (r1 - mid.astype(F32)).astype(BF16)
    return hi, mid, lo


def _dot(a, b):
    return jnp.dot(a, b, preferred_element_type=F32)


def _dot3(x, m):
    hi, mid, lo = _split3(x)
    return _dot(hi, m) + _dot(mid, m) + _dot(lo, m)


def _dot3_left(m, x):
    hi, mid, lo = _split3(x)
    return _dot(m, hi) + _dot(m, mid) + _dot(m, lo)


def _log_sigmoid(x):
    return jnp.minimum(x, 0.0) - jnp.log1p(jnp.exp(-jnp.abs(x)))


def _sigmoid(x):
    return 1.0 / (1.0 + jnp.exp(-x))


def _softplus(x):
    return jnp.maximum(x, 0.0) + jnp.log1p(jnp.exp(-jnp.abs(x)))


def _gelu(x):
    return jax.nn.gelu(x, approximate=True)


def _head_ones(width, head):
    r = lax.broadcasted_iota(jnp.int32, (width, width), 0) // head
    c = lax.broadcasted_iota(jnp.int32, (width, width), 1) // head
    return (r == c).astype(BF16)


def _even_in_kernel(h_ref, g_ref, w_ref, bf_ref, q_ref, k_ref, v_ref, u_ref, lf_ref):
    xn = _rms(h_ref[...], g_ref[...]).astype(BF16)
    z = _dot(xn, w_ref[...])
    q_ref[...] = (z[:, :A_W] * (DH_A ** -0.5)).astype(BF16)
    k_ref[...] = z[:, A_W:2 * A_W]
    v_ref[...] = z[:, 2 * A_W:3 * A_W]
    u_ref[...] = z[:, 3 * A_W:3 * A_W + B_W]
    lf_ref[...] = _log_sigmoid(z[:, 3 * A_W + B_W:] + bf_ref[...])


def _even_in(h, g, w, bf):
    m = h.shape[0]
    tm = _tile(m, 512)
    n = w.shape[1]
    row = lambda width: pl.BlockSpec((tm, width), lambda i: (i, 0))
    return pl.pallas_call(
        _even_in_kernel,
        grid=(m // tm,),
        in_specs=[row(D_MODEL), _const_spec((1, D_MODEL)), _const_spec((D_MODEL, n)), _const_spec((1, LANE))],
        out_specs=[row(A_W), row(A_W), row(A_W), row(B_W), row(LANE)],
        out_shape=[jax.ShapeDtypeStruct((m, A_W), BF16), jax.ShapeDtypeStruct((m, A_W), F32),
                   jax.ShapeDtypeStruct((m, A_W), F32), jax.ShapeDtypeStruct((m, B_W), F32),
                   jax.ShapeDtypeStruct((m, LANE), F32)],
        compiler_params=_cp("parallel"),
        name="even_in",
    )(h, g, w, bf)


def _cumsum_kernel(x_ref, c_ref, hi_ref, mid_ref, lo_ref, carry_ref, *, tseq):
    tc = x_ref.shape[1]
    ri = lax.broadcasted_iota(jnp.int32, (tc, tc), 0)
    ci = lax.broadcasted_iota(jnp.int32, (tc, tc), 1)
    if tseq >= tc:
        tri = (ri >= ci).astype(BF16)

        @pl.when((pl.program_id(1) * tc) % tseq == 0)
        def _():
            carry_ref[...] = jnp.zeros_like(carry_ref)
    else:
        tri = ((ri >= ci) & (ri // tseq == ci // tseq)).astype(BF16)
    c = _dot3_left(tri, x_ref[0])
    if tseq >= tc:
        c = c + carry_ref[...]
        carry_ref[...] = c[tc - 1:tc, :]
    c_ref[0] = c
    hi, mid, lo = _split3(c)
    hi_ref[0] = hi
    mid_ref[0] = mid
    lo_ref[0] = lo


def _cumsum(x, tseq):
    g, length, _ = x.shape
    tc = _tile(length, 512)
    assert tseq % tc == 0 or tc % tseq == 0
    blk = pl.BlockSpec((1, tc, LANE), lambda i, j: (i, j, 0))
    return pl.pallas_call(
        functools.partial(_cumsum_kernel, tseq=tseq),
        grid=(g, length // tc),
        in_specs=[blk],
        out_specs=[blk] * 4,
        out_shape=[jax.ShapeDtypeStruct(x.shape, F32)] + [jax.ShapeDtypeStruct(x.shape, BF16)] * 3,
        scratch_shapes=[pltpu.VMEM((1, LANE), F32)],
        compiler_params=_cp("arbitrary", "arbitrary"),
        name="fox_cumsum",
    )(x)


def _flash_kernel(q_ref, k_ref, v_ref, o_ref, *, rpt, tk, scale):
    rows = q_ref.shape[1]
    dv = v_ref.shape[2]
    tq = rows // rpt
    t0 = pl.program_id(1) * tq
    q = q_ref[0]

    def chunk(kc, carry, masked):
        m, l, acc = carry
        k0 = pl.multiple_of(kc * tk, tk)
        kb = k_ref[0, pl.ds(k0, tk), :]
        vb = v_ref[0, pl.ds(k0, tk), :]
        s = lax.dot_general(q, kb, (((1,), (1,)), ((), ())), preferred_element_type=F32)
        if scale != 1.0:
            s = s * scale
        if masked:
            tok = t0 + lax.broadcasted_iota(jnp.int32, (rows, tk), 0) // rpt
            key = k0 + lax.broadcasted_iota(jnp.int32, (rows, tk), 1)
            s = jnp.where(key <= tok, s, NEG)
        m_new = jnp.maximum(m, jnp.max(s, axis=1, keepdims=True))
        alpha = jnp.exp(m - m_new)
        p = jnp.exp(s - m_new)
        l = alpha * l + jnp.sum(p, axis=1, keepdims=True)
        acc = alpha * acc + _dot(p.astype(BF16), vb)
        return m_new, l, acc

    init = (jnp.full((rows, 1), NEG, F32), jnp.zeros((rows, 1), F32), jnp.zeros((rows, dv), F32))
    n_full = t0 // tk
    carry = lax.fori_loop(0, n_full, lambda kc, c: chunk(kc, c, False), init)
    for j in range(max(1, tq // tk)):
        carry = chunk(n_full + j, carry, True)
    _, l, acc = carry
    o_ref[0] = (acc / l).astype(o_ref.dtype)


def _flash(q, k, v, *, rpt, tq, tk, scale):
    g, rows_total, dq = q.shape
    s = k.shape[1]
    dv = v.shape[2]
    tq = min(tq, s)
    tk = min(tk, s)
    assert s % tq == 0 and s % tk == 0 and (tk % tq == 0 or tq % tk == 0)
    rows = tq * rpt
    return pl.pallas_call(
        functools.partial(_flash_kernel, rpt=rpt, tk=tk, scale=scale),
        grid=(g, s // tq),
        in_specs=[pl.BlockSpec((1, rows, dq), lambda i, j: (i, j, 0)),
                  pl.BlockSpec((1, s, dq), lambda i, j: (i, 0, 0)),
                  pl.BlockSpec((1, s, dv), lambda i, j: (i, 0, 0))],
        out_specs=pl.BlockSpec((1, rows, dv), lambda i, j: (i, j, 0)),
        out_shape=jax.ShapeDtypeStruct((g, rows_total, dv), BF16),
        compiler_params=_cp("parallel", "arbitrary"),
        name="flash_prompt",
    )(q, k, v)


def _online_update(s, vals, m_ref, l_ref, acc_ref):
    m_old = m_ref[...]
    m_new = m_old
    for blk in s:
        m_new = jnp.maximum(m_new, jnp.max(blk, axis=1, keepdims=True))
    alpha = jnp.exp(m_old - m_new)
    l = alpha * l_ref[...]
    acc = alpha * acc_ref[...]
    for blk, val in zip(s, vals):
        p = jnp.exp(blk - m_new)
        l = l + jnp.sum(p, axis=1, keepdims=True)
        acc = acc + _dot(p.astype(BF16), val)
    m_ref[...] = m_new
    l_ref[...] = l
    acc_ref[...] = acc


def _new_key_mask(rows, t_new):
    tok = lax.broadcasted_iota(jnp.int32, (rows, LANE), 0) // (rows // t_new)
    key = lax.broadcasted_iota(jnp.int32, (rows, LANE), 1)
    return key <= tok


def _fox_sample_kernel(pt_ref, q_ref, kn_ref, vn_ref, cnr_ref, cnk_ref, *rest, npg, t_new):
    k_refs = rest[:npg]
    v_refs = rest[npg:2 * npg]
    lf_refs = rest[2 * npg:3 * npg]
    o_ref, m_ref, l_ref, acc_ref, carry_ref = rest[3 * npg:]
    c = pl.program_id(1)
    rows = q_ref.shape[1]
    q = q_ref[0]
    cn_rows = cnr_ref[0]
    nt = (((1,), (1,)), ((), ()))

    @pl.when(c == 0)
    def _():
        m_ref[...] = jnp.full_like(m_ref, NEG)
        l_ref[...] = jnp.zeros_like(l_ref)
        acc_ref[...] = jnp.zeros_like(acc_ref)
        carry_ref[...] = jnp.zeros_like(carry_ref)
        s = lax.dot_general(q, kn_ref[0], nt, preferred_element_type=F32)
        s = s + cn_rows - jnp.tile(cnk_ref[0], (t_new, 1))
        s = jnp.where(_new_key_mask(rows, t_new), s, NEG)
        _online_update([s], [vn_ref[0]], m_ref, l_ref, acc_ref)

    ri = lax.broadcasted_iota(jnp.int32, (LANE, LANE), 0)
    ci = lax.broadcasted_iota(jnp.int32, (LANE, LANE), 1)
    after = (ri > ci).astype(BF16)
    ones = jnp.ones((LANE, LANE), BF16)
    lf_all = jnp.concatenate([r[...] for r in lf_refs], axis=0)
    suffix = _dot3(lf_all, after)
    total = _dot3(lf_all, ones)
    carry = carry_ref[...]
    scores = [None] * npg
    for i in reversed(range(npg)):
        kb = k_refs[i][...].astype(BF16)
        s = lax.dot_general(q, kb, nt, preferred_element_type=F32)
        bias = suffix[i * H_A:(i + 1) * H_A] + carry
        scores[i] = s + cn_rows + jnp.tile(bias, (t_new, 1))
        carry = carry + total[i * H_A:(i + 1) * H_A]
    carry_ref[...] = carry
    _online_update(scores, [r[...].astype(BF16) for r in v_refs], m_ref, l_ref, acc_ref)

    @pl.when(c == pl.num_programs(1) - 1)
    def _():
        out = acc_ref[...] / l_ref[...]
        rh = lax.broadcasted_iota(jnp.int32, out.shape, 0) % H_A
        ch = lax.broadcasted_iota(jnp.int32, out.shape, 1) // DH_A
        out = jnp.where(rh == ch, out, 0.0).reshape(t_new, H_A, A_W)
        o_ref[0] = jnp.sum(out, axis=1).astype(o_ref.dtype)


def _fox_sample(layer, page_table, q_bd, k_new, v_new, cn_rows, cn_keys, cache_k, cache_v, cache_lf_t):
    b, n_pages = page_table.shape
    page = cache_k.shape[2]
    assert page == LANE and n_pages % PAGES_PER_STEP == 0
    npg = PAGES_PER_STEP
    n_chunks = n_pages // npg
    rows = q_bd.shape[1]
    t_new = rows // H_A

    def seq_spec(shape):
        return pl.BlockSpec((1,) + shape, lambda i, c, pt: (i, 0, 0))

    def page_spec(shape, idx):
        def imap(i, c, pt):
            return (layer, pt[i, (n_chunks - 1 - c) * npg + idx], 0, 0)
        return pl.BlockSpec((None, None) + shape, imap)

    in_specs = [seq_spec((rows, A_W)), seq_spec((LANE, A_W)), seq_spec((LANE, A_W)),
                seq_spec((rows, LANE)), seq_spec((H_A, LANE))]
    in_specs += [page_spec((page, A_W), i) for i in range(npg)]
    in_specs += [page_spec((page, A_W), i) for i in range(npg)]
    in_specs += [page_spec((H_A, page), i) for i in range(npg)]
    grid_spec = pltpu.PrefetchScalarGridSpec(
        num_scalar_prefetch=1, grid=(b, n_chunks), in_specs=in_specs,
        out_specs=pl.BlockSpec((1, t_new, A_W), lambda i, c, pt: (i, 0, 0)),
        scratch_shapes=[pltpu.VMEM((rows, 1), F32), pltpu.VMEM((rows, 1), F32),
                        pltpu.VMEM((rows, A_W), F32), pltpu.VMEM((H_A, LANE), F32)])
    return pl.pallas_call(
        functools.partial(_fox_sample_kernel, npg=npg, t_new=t_new),
        grid_spec=grid_spec,
        out_shape=jax.ShapeDtypeStruct((b, t_new, A_W), BF16),
        compiler_params=_cp("parallel", "arbitrary"),
        name="fox_sample",
    )(page_table, q_bd, k_new, v_new, cn_rows, cn_keys,
      *([cache_k] * npg), *([cache_v] * npg), *([cache_lf_t] * npg))


def _mla_sample_kernel(pt_ref, qa_ref, qp_ref, cn_ref, kn_ref, *rest, npg, t_new, scale):
    c_refs = rest[:npg]
    p_refs = rest[npg:2 * npg]
    o_ref, m_ref, l_ref, acc_ref = rest[2 * npg:]
    c = pl.program_id(1)
    rows = qa_ref.shape[1]
    qa = qa_ref[0]
    qp = qp_ref[0]
    nt = (((1,), (1,)), ((), ()))

    def score(cb, pb):
        s = (lax.dot_general(qa, cb, nt, preferred_element_type=F32)
             + lax.dot_general(qp, pb, nt, preferred_element_type=F32))
        return s * scale

    @pl.when(c == 0)
    def _():
        m_ref[...] = jnp.full_like(m_ref, NEG)
        l_ref[...] = jnp.zeros_like(l_ref)
        acc_ref[...] = jnp.zeros_like(acc_ref)
        s = jnp.where(_new_key_mask(rows, t_new), score(cn_ref[0], kn_ref[0]), NEG)
        _online_update([s], [cn_ref[0]], m_ref, l_ref, acc_ref)

    cbs = [r[...].astype(BF16) for r in c_refs]
    scores = [score(cb, r[...].astype(BF16)) for cb, r in zip(cbs, p_refs)]
    _online_update(scores, cbs, m_ref, l_ref, acc_ref)

    @pl.when(c == pl.num_programs(1) - 1)
    def _():
        o_ref[0] = (acc_ref[...] / l_ref[...]).astype(o_ref.dtype)


def _mla_sample(layer, page_table, q_abs, q_pe, c_new, kpe_new, cache_ckv, cache_kpe):
    b, n_pages = page_table.shape
    page = cache_ckv.shape[2]
    assert page == LANE and n_pages % PAGES_PER_STEP == 0
    npg = PAGES_PER_STEP
    n_chunks = n_pages // npg
    rows = q_abs.shape[1]
    t_new = rows // H_D

    def seq_spec(shape):
        return pl.BlockSpec((1,) + shape, lambda i, c, pt: (i, 0, 0))

    def page_spec(shape, idx):
        def imap(i, c, pt):
            return (layer, pt[i, c * npg + idx], 0, 0)
        return pl.BlockSpec((None, None) + shape, imap)

    in_specs = [seq_spec((rows, KV_LORA)), seq_spec((rows, ROPE_D)),
                seq_spec((LANE, KV_LORA)), seq_spec((LANE, ROPE_D))]
    in_specs += [page_spec((page, KV_LORA), i) for i in range(npg)]
    in_specs += [page_spec((page, ROPE_D), i) for i in range(npg)]
    grid_spec = pltpu.PrefetchScalarGridSpec(
        num_scalar_prefetch=1, grid=(b, n_chunks), in_specs=in_specs,
        out_specs=pl.BlockSpec((1, rows, KV_LORA), lambda i, c, pt: (i, 0, 0)),
        scratch_shapes=[pltpu.VMEM((rows, 1), F32), pltpu.VMEM((rows, 1), F32),
                        pltpu.VMEM((rows, KV_LORA), F32)])
    return pl.pallas_call(
        functools.partial(_mla_sample_kernel, npg=npg, t_new=t_new, scale=(NOPE_D + ROPE_D) ** -0.5),
        grid_spec=grid_spec,
        out_shape=jax.ShapeDtypeStruct((b, rows, KV_LORA), BF16),
        compiler_params=_cp("parallel", "arbitrary"),
        name="mla_sample",
    )(page_table, q_abs, q_pe, c_new, kpe_new, *([cache_ckv] * npg), *([cache_kpe] * npg))


def _s5_prep_kernel(lr_ref, li_ref, ldt_ref, br_ref, bi_ref, ab_ref, bbr_ref, bbi_ref):
    lr = lr_ref[...]
    li = li_ref[...]
    dt = jnp.exp(ldt_ref[...])
    mag = jnp.exp(lr * dt)
    ang = li * dt
    ab_re = mag * jnp.cos(ang)
    ab_im = mag * jnp.sin(ang)
    den = lr * lr + li * li
    f_re = ((ab_re - 1.0) * lr + ab_im * li) / den
    f_im = (ab_im * lr - (ab_re - 1.0) * li) / den
    ab_ref[0:1, :] = ab_re
    ab_ref[1:2, :] = ab_im
    br = br_ref[...]
    bi = bi_ref[...]
    bbr_ref[...] = f_re * br - f_im * bi
    bbi_ref[...] = f_re * bi + f_im * br


def _s5_prep(lam_re, lam_im, log_dt, b_re_t, b_im_t):
    return pl.pallas_call(
        _s5_prep_kernel,
        out_shape=[jax.ShapeDtypeStruct((2, S5_W), F32), jax.ShapeDtypeStruct((S5_P, S5_W), F32),
                   jax.ShapeDtypeStruct((S5_P, S5_W), F32)],
        name="s5_prep",
    )(lam_re, lam_im, log_dt, b_re_t, b_im_t)


def _cmul(ar, ai, br, bi):
    return ar * br - ai * bi, ar * bi + ai * br


def _s5_kernel(u_ref, ab_ref, bdr_ref, bdi_ref, cdr_ref, cdi_ref, d_ref, wg_ref, bg_ref, h0r_ref, h0i_ref,
               y_ref, sr_ref, si_ref, xr_ref, xi_ref, car_ref, cai_ref, *, grouped):
    tc = u_ref.shape[0]
    u = u_ref[...]
    ub = u.astype(BF16)
    xr_ref[...] = _dot(ub, bdr_ref[...])
    xi_ref[...] = _dot(ub, bdi_ref[...])

    if not grouped:
        @pl.when(pl.program_id(1) == 0)
        def _():
            car_ref[...] = h0r_ref[0]
            cai_ref[...] = h0i_ref[0]

    cw = S5_SCAN_LANES
    tile = (SUBLANE, cw)
    row = lax.broadcasted_iota(jnp.int32, tile, 0)
    n_groups = tc // SUBLANE
    for j in range(S5_W // cw):
        lanes = slice(j * cw, (j + 1) * cw)
        a1r = jnp.broadcast_to(ab_ref[0:1, lanes], tile)
        a1i = jnp.broadcast_to(ab_ref[1:2, lanes], tile)
        a2r, a2i = _cmul(a1r, a1i, a1r, a1i)
        a4r, a4i = _cmul(a2r, a2i, a2r, a2i)
        apr, api = a1r, a1i
        for d, (pr, pi) in ((1, (a1r, a1i)), (2, (a2r, a2i)), (4, (a4r, a4i))):
            sr = pltpu.roll(apr, d, axis=0)
            si = pltpu.roll(api, d, axis=0)
            mr, mi = _cmul(pr, pi, sr, si)
            apr = jnp.where(row >= d, mr, apr)
            api = jnp.where(row >= d, mi, api)

        def body(g, carry):
            cr, ci = carry
            r0 = pl.multiple_of(g * SUBLANE, SUBLANE)
            xr = xr_ref[pl.ds(r0, SUBLANE), lanes]
            xi = xi_ref[pl.ds(r0, SUBLANE), lanes]
            for d, (pr, pi) in ((1, (a1r, a1i)), (2, (a2r, a2i)), (4, (a4r, a4i))):
                sr = jnp.where(row >= d, pltpu.roll(xr, d, axis=0), 0.0)
                si = jnp.where(row >= d, pltpu.roll(xi, d, axis=0), 0.0)
                mr, mi = _cmul(pr, pi, sr, si)
                xr = xr + mr
                xi = xi + mi
            if grouped:
                cr = jnp.broadcast_to(h0r_ref[pl.ds(g, 1), lanes], tile)
                ci = jnp.broadcast_to(h0i_ref[pl.ds(g, 1), lanes], tile)
            mr, mi = _cmul(apr, api, cr, ci)
            xr = xr + mr
            xi = xi + mi
            xr_ref[pl.ds(r0, SUBLANE), lanes] = xr
            xi_ref[pl.ds(r0, SUBLANE), lanes] = xi
            if grouped:
                sr_ref[pl.ds(g, 1), lanes] = xr[SUBLANE - 1:SUBLANE]
                si_ref[pl.ds(g, 1), lanes] = xi[SUBLANE - 1:SUBLANE]
                return cr, ci
            last_r = jnp.broadcast_to(xr[SUBLANE - 1:SUBLANE], tile)
            last_i = jnp.broadcast_to(xi[SUBLANE - 1:SUBLANE], tile)
            return last_r, last_i

        if grouped:
            init = (jnp.zeros(tile, F32), jnp.zeros(tile, F32))
        else:
            init = (jnp.broadcast_to(car_ref[0:1, lanes], tile), jnp.broadcast_to(cai_ref[0:1, lanes], tile))
        cr, ci = lax.fori_loop(0, n_groups, body, init)
        if not grouped:
            car_ref[0:1, lanes] = cr[0:1]
            cai_ref[0:1, lanes] = ci[0:1]
            sr_ref[0, 0:1, lanes] = cr[0:1]
            si_ref[0, 0:1, lanes] = ci[0:1]

    y = (_dot(xr_ref[...].astype(BF16), cdr_ref[...]) - _dot(xi_ref[...].astype(BF16), cdi_ref[...])
         + d_ref[...] * u)
    y = _gelu(y)
    y = y * _sigmoid(_dot(y.astype(BF16), wg_ref[...]) + bg_ref[...])
    y_ref[...] = y.astype(y_ref.dtype)


def _s5(u, ab, bd_re, bd_im, cd_re, cd_im, d, w_glu, b_glu, h0_re, h0_im, *, n_seq, tseq):
    m = u.shape[0]
    grouped = tseq == SUBLANE
    consts = [_const_spec((2, S5_W)), _const_spec((B_W, S5_W)), _const_spec((B_W, S5_W)),
              _const_spec((S5_W, B_W)), _const_spec((S5_W, B_W)), _const_spec((1, B_W)),
              _const_spec((B_W, B_W)), _const_spec((1, B_W))]
    if grouped:
        tc = _tile(m, 256)
        ng = tc // SUBLANE
        grid = (m // tc, 1)
        u_spec = pl.BlockSpec((tc, B_W), lambda i, j: (i, 0))
        st_spec = pl.BlockSpec((ng, S5_W), lambda i, j: (i, 0))
        h0_spec = st_spec
        st_shape = jax.ShapeDtypeStruct((n_seq, S5_W), F32)
    else:
        tc = _tile(tseq, 256)
        assert tc % SUBLANE == 0
        nt = tseq // tc
        grid = (n_seq, nt)
        u_spec = pl.BlockSpec((tc, B_W), lambda i, j: (i * nt + j, 0))
        st_spec = pl.BlockSpec((1, 1, S5_W), lambda i, j: (i, 0, 0))
        h0_spec = st_spec
        st_shape = jax.ShapeDtypeStruct((n_seq, 1, S5_W), F32)
        h0_re = h0_re.reshape(n_seq, 1, S5_W)
        h0_im = h0_im.reshape(n_seq, 1, S5_W)
    y, s_re, s_im = pl.pallas_call(
        functools.partial(_s5_kernel, grouped=grouped),
        grid=grid,
        in_specs=[u_spec] + consts + [h0_spec, h0_spec],
        out_specs=[u_spec, st_spec, st_spec],
        out_shape=[jax.ShapeDtypeStruct((m, B_W), BF16), st_shape, st_shape],
        scratch_shapes=[pltpu.VMEM((tc, S5_W), F32), pltpu.VMEM((tc, S5_W), F32),
                        pltpu.VMEM((1, S5_W), F32), pltpu.VMEM((1, S5_W), F32)],
        compiler_params=_cp("arbitrary", "arbitrary"),
        name="s5_mix",
    )(u, ab, bd_re, bd_im, cd_re, cd_im, d, w_glu, b_glu, h0_re, h0_im)
    return y, s_re.reshape(n_seq, S5_W), s_im.reshape(n_seq, S5_W)


def _even_out_kernel(h_ref, a_ref, b_ref, w_ref, o_ref):
    o_ref[...] = h_ref[...] + _dot(a_ref[...], w_ref[0:A_W, :]) + _dot(b_ref[...], w_ref[A_W:, :])


def _odd_out_kernel(h_ref, a_ref, lat_ref, bdv_ref, w_ref, o_ref):
    o_d = _dot(lat_ref[...], bdv_ref[...]).astype(BF16)
    o_ref[...] = h_ref[...] + _dot(a_ref[...], w_ref[0:C_W, :]) + _dot(o_d, w_ref[C_W:, :])


def _even_out(h, a, b, w):
    m = h.shape[0]
    tm = _tile(m, 512)
    row = lambda width: pl.BlockSpec((tm, width), lambda i: (i, 0))
    return pl.pallas_call(
        _even_out_kernel, grid=(m // tm,),
        in_specs=[row(D_MODEL), row(A_W), row(B_W), _const_spec(w.shape)],
        out_specs=row(D_MODEL), out_shape=jax.ShapeDtypeStruct((m, D_MODEL), F32),
        compiler_params=_cp("parallel"), name="even_out",
    )(h, a, b, w)


def _odd_out(h, a, lat, bdv, w):
    m = h.shape[0]
    tm = _tile(m, 512)
    row = lambda width: pl.BlockSpec((tm, width), lambda i: (i, 0))
    return pl.pallas_call(
        _odd_out_kernel, grid=(m // tm,),
        in_specs=[row(D_MODEL), row(C_W), row(H_D * KV_LORA), _const_spec(bdv.shape), _const_spec(w.shape)],
        out_specs=row(D_MODEL), out_shape=jax.ShapeDtypeStruct((m, D_MODEL), F32),
        compiler_params=_cp("parallel"), name="odd_out",
    )(h, a, lat, bdv, w)


def _shifted(x, d, tseq, tails):
    tm = x.shape[0]
    rolled = pltpu.roll(x, d, axis=0)
    row = lax.broadcasted_iota(jnp.int32, (tm, 1), 0)
    if tseq >= tm:
        taps = tails.shape[0]
        out = rolled
        for i in range(d):
            out = jnp.where(row == i, tails[taps - d + i:taps - d + i + 1, :], out)
        return out
    taps = len(tails)
    pos = row % tseq
    out = rolled
    for i in range(d):
        out = jnp.where(pos == i, tails[taps - d + i], out)
    return out


def _odd_in_kernel(h_ref, g_ref, w_ref, zc_ref, zd_ref):
    xn = _rms(h_ref[...], g_ref[...]).astype(BF16)
    z = _dot(xn, w_ref[...])
    zc_ref[...] = z[:, :C_PAD]
    zd_ref[...] = z[:, C_PAD:]


def _odd_in(h, g, w):
    m = h.shape[0]
    tm = _tile(m, 512)
    row = lambda width: pl.BlockSpec((tm, width), lambda i: (i, 0))
    return pl.pallas_call(
        _odd_in_kernel, grid=(m // tm,),
        in_specs=[row(D_MODEL), _const_spec((1, D_MODEL)), _const_spec(w.shape)],
        out_specs=[row(C_PAD), row(D_PAD)],
        out_shape=[jax.ShapeDtypeStruct((m, C_PAD), F32), jax.ShapeDtypeStruct((m, D_PAD), F32)],
        compiler_params=_cp("parallel"), name="odd_in",
    )(h, g, w)


def _rwkv_pre_kernel(z_ref, tail_ref, mu_ref, w0_ref, w2_ref, a0_ref, a2_ref, g2_ref, kkw_ref, kaw_ref,
                     rk_ref, r_ref, w_ref, k_ref, v_ref, kk_ref, kka_ref, g_ref, bonus_ref, carry_ref,
                     *, tseq):
    tm = z_ref.shape[0]
    z = z_ref[...]
    if tseq >= tm:
        @pl.when((pl.program_id(0) * tm) % tseq == 0)
        def _():
            carry_ref[...] = tail_ref[0]
        zprev = _shifted(z, 1, tseq, carry_ref[...])
        carry_ref[...] = z[tm - 1:tm, :]
    else:
        zprev = _shifted(z, 1, tseq, [tail_ref[...]])
    zm = z + (zprev - z) * mu_ref[...]
    r = zm[:, :C_W]
    k = zm[:, C_W:2 * C_W]
    v = zm[:, 2 * C_W:3 * C_W]
    wd = zm[:, 3 * C_W:3 * C_W + LANE]
    ad = zm[:, 3 * C_W + LANE:3 * C_W + 2 * LANE]
    gd = zm[:, 3 * C_W + 2 * LANE:]
    w_log = -_softplus(-(w0_ref[...] + _dot(jnp.tanh(wd).astype(BF16), w2_ref[...]))) - 0.5
    decay = jnp.exp(-jnp.exp(w_log))
    a = _sigmoid(a0_ref[...] + _dot(ad.astype(BF16), a2_ref[...]))
    g = _dot(_sigmoid(gd).astype(BF16), g2_ref[...])
    ones = _head_ones(C_W, DH_C)
    kk = k * kkw_ref[...]
    norm = jnp.sqrt(_dot3(kk * kk, ones))
    kk = kk / jnp.maximum(norm, 1e-12)
    k2 = k * (1.0 + (a - 1.0) * kaw_ref[...])
    r_ref[...] = r
    w_ref[...] = decay
    k_ref[...] = k2
    v_ref[...] = v
    kk_ref[...] = kk
    kka_ref[...] = kk * a
    g_ref[...] = g
    bonus_ref[...] = _dot3(r * k2 * rk_ref[...], ones) * v


def _rwkv_pre(z_c, tails, p, *, tseq):
    m = z_c.shape[0]
    tm = _tile(min(m, tseq) if tseq >= 256 else m, 256)
    row = lambda width: pl.BlockSpec((tm, width), lambda i: (i, 0))
    if tseq >= tm:
        nt = tseq // tm
        tail_spec = pl.BlockSpec((1, 1, C_PAD), lambda i: (i // nt, 0, 0))
    else:
        tail_spec = row(C_PAD)
    vec = _const_spec((1, C_W))
    return pl.pallas_call(
        functools.partial(_rwkv_pre_kernel, tseq=tseq), grid=(m // tm,),
        in_specs=[row(C_PAD), tail_spec, _const_spec((1, C_PAD)), vec, _const_spec((LANE, C_W)), vec,
                  _const_spec((LANE, C_W)), _const_spec((LANE, C_W)), vec, vec, vec],
        out_specs=[row(C_W)] * 8,
        out_shape=[jax.ShapeDtypeStruct((m, C_W), F32)] * 8,
        scratch_shapes=[pltpu.VMEM((1, C_PAD), F32)],
        compiler_params=_cp("arbitrary"), name="rwkv_pre",
    )(z_c, tails, p["mu"], p["w0"], p["w2"], p["a0"], p["a2"], p["g2"], p["kk"], p["ka"], p["rk"])


def _rwkv_scan_kernel(r_ref, w_ref, k_ref, v_ref, kk_ref, kka_ref, s0_ref, y_ref, so_ref, s_ref):
    tc = r_ref.shape[0]

    @pl.when(pl.program_id(1) == 0)
    def _():
        s_ref[...] = s0_ref[0]

    eye = (lax.broadcasted_iota(jnp.int32, (DH_C, DH_C), 0)
           == lax.broadcasted_iota(jnp.int32, (DH_C, DH_C), 1)).astype(F32)

    def step(t, _):
        row = pl.ds(t, 1)
        r_t, w_t, k_t, v_t, kk_t, kka_t = (x[row, :] for x in (r_ref, w_ref, k_ref, v_ref, kk_ref, kka_ref))
        ys = []
        for h in range(H_C):
            sl = slice(h * DH_C, (h + 1) * DH_C)
            s = s_ref[h]
            sk = jnp.sum(s * kk_t[:, sl], axis=1, keepdims=True)
            v_c = jnp.sum(eye * v_t[:, sl], axis=1, keepdims=True)
            s = s * w_t[:, sl] - sk * kka_t[:, sl] + v_c * k_t[:, sl]
            s_ref[h] = s
            y_c = jnp.sum(s * r_t[:, sl], axis=1, keepdims=True)
            ys.append(jnp.sum(eye * y_c, axis=0, keepdims=True))
        y_ref[row, :] = jnp.concatenate(ys, axis=1)
        return 0

    lax.fori_loop(0, tc, step, 0)
    so_ref[0] = s_ref[...]


def _rwkv_scan(r, w, k, v, kk, kka, s0, *, tseq):
    m = r.shape[0]
    n_seq = m // tseq
    tc = _tile(tseq, 256)
    nt = tseq // tc
    row = pl.BlockSpec((tc, C_W), lambda i, j: (i * nt + j, 0))
    st = pl.BlockSpec((1, H_C, DH_C, DH_C), lambda i, j: (i, 0, 0, 0))
    return pl.pallas_call(
        _rwkv_scan_kernel, grid=(n_seq, nt),
        in_specs=[row] * 6 + [st],
        out_specs=[row, st],
        out_shape=[jax.ShapeDtypeStruct((m, C_W), F32), jax.ShapeDtypeStruct(s0.shape, F32)],
        scratch_shapes=[pltpu.VMEM((H_C, DH_C, DH_C), F32)],
        compiler_params=_cp("parallel", "arbitrary"), name="rwkv_scan",
    )(r, w, k, v, kk, kka, s0)


def _rwkv_post_kernel(y_ref, bonus_ref, g_ref, lnw_ref, lnb_ref, o_ref):
    y = y_ref[...]
    mean_mat = _head_ones(C_W, DH_C)
    mu = _dot3(y, mean_mat) * (1.0 / DH_C)
    yc = y - mu
    var = _dot3(yc * yc, mean_mat) * (1.0 / DH_C)
    yn = yc * lax.rsqrt(var + RWKV_GN_EPS) * lnw_ref[...] + lnb_ref[...]
    o_ref[...] = ((yn + bonus_ref[...]) * g_ref[...]).astype(o_ref.dtype)


def _rwkv_post(y, bonus, g, lnw, lnb):
    m = y.shape[0]
    tm = _tile(m, 512)
    row = pl.BlockSpec((tm, C_W), lambda i: (i, 0))
    vec = _const_spec((1, C_W))
    return pl.pallas_call(
        _rwkv_post_kernel, grid=(m // tm,),
        in_specs=[row, row, row, vec, vec], out_specs=row,
        out_shape=jax.ShapeDtypeStruct((m, C_W), BF16),
        compiler_params=_cp("parallel"), name="rwkv_post",
    )(y, bonus, g, lnw, lnb)


def _mla_pre_kernel(z_ref, qn_ref, kvn_ref, wq_ref, bdk_ref, cq_ref, sq_ref, ck_ref, sk_ref,
                    qa_ref, qp_ref, c_ref, kpe_ref):
    z = z_ref[...]
    qn = _rms(z[:, :Q_LORA], qn_ref[...]).astype(BF16)
    qf = _dot(qn, wq_ref[...])
    n_nope = H_D * NOPE_D
    n_pe = H_D * ROPE_D
    q_pe = qf[:, n_nope:n_nope + n_pe] * cq_ref[...] + qf[:, n_nope + n_pe:] * sq_ref[...]
    qa_ref[...] = _dot(qf[:, :n_nope].astype(BF16), bdk_ref[...]).astype(BF16)
    qp_ref[...] = q_pe.astype(BF16)
    c_ref[...] = _rms(z[:, Q_LORA:Q_LORA + KV_LORA], kvn_ref[...])
    o = Q_LORA + KV_LORA
    kpe_ref[...] = z[:, o:o + ROPE_D] * ck_ref[...] + z[:, o + ROPE_D:o + 2 * ROPE_D] * sk_ref[...]


def _mla_pre(z_d, p, cos_q, sin_q, cos_k, sin_k, *, tseq):
    m = z_d.shape[0]
    if tseq < 256:
        cos_q, sin_q, cos_k, sin_k = (jnp.tile(t, (m // tseq, 1)) for t in (cos_q, sin_q, cos_k, sin_k))
        tseq = m
    tm = _tile(min(m, tseq), 512)
    nt = tseq // tm
    row = lambda width: pl.BlockSpec((tm, width), lambda i: (i, 0))
    pos = lambda width: pl.BlockSpec((tm, width), lambda i: (i % nt, 0))
    n_pe = H_D * ROPE_D
    return pl.pallas_call(
        _mla_pre_kernel, grid=(m // tm,),
        in_specs=[row(D_PAD), _const_spec((1, Q_LORA)), _const_spec((1, KV_LORA)), _const_spec(p["wq"].shape),
                  _const_spec(p["bdk"].shape), pos(n_pe), pos(n_pe), pos(ROPE_D), pos(ROPE_D)],
        out_specs=[row(H_D * KV_LORA), row(n_pe), row(KV_LORA), row(ROPE_D)],
        out_shape=[jax.ShapeDtypeStruct((m, H_D * KV_LORA), BF16), jax.ShapeDtypeStruct((m, n_pe), BF16),
                   jax.ShapeDtypeStruct((m, KV_LORA), F32), jax.ShapeDtypeStruct((m, ROPE_D), F32)],
        compiler_params=_cp("parallel"), name="mla_pre",
    )(z_d, p["q_norm"], p["kv_norm"], p["wq"], p["bdk"], cos_q, sin_q, cos_k, sin_k)


def _ffn_kernel(h_ref, p_ref, tail_ref, *rest, tseq, final, n_tails):
    tail_refs = (tail_ref,) + rest[:n_tails - 1]
    (gf_ref, wup_ref, cw_ref, cb_ref, wdn_ref, gp_ref, wpg_ref, bpg_ref, wple_ref, gfin_ref,
     o_ref, buf_ref, carry_ref) = rest[n_tails - 1:]
    tm = h_ref.shape[0]
    taps = CONV_W - 1
    h = h_ref[...]
    xn = _rms(h, gf_ref[...]).astype(BF16)
    up = _dot(xn, wup_ref[...])
    a = up[:, :D_FF]
    b = up[:, D_FF:]
    if tseq >= tm:
        @pl.when((pl.program_id(0) * tm) % tseq == 0)
        def _():
            carry_ref[...] = tail_refs[0][0]
        tails = carry_ref[...]
    else:
        tails = [r[...] for r in tail_refs]
    c = cb_ref[...] + cw_ref[taps:taps + 1, :] * a
    for d in range(1, CONV_W):
        c = c + cw_ref[taps - d:taps - d + 1, :] * _shifted(a, d, tseq, tails)
    if tseq >= tm:
        carry_ref[...] = a[tm - taps:, :]
        buf_ref[0] = a[tm - taps:, :]
    else:
        buf_ref[...] = a.reshape(tm // tseq, tseq, D_FF)[:, tseq - taps:, :]
    f = _dot((_gelu(c) * b).astype(BF16), wdn_ref[...])
    h = h + f
    gate = _sigmoid(_dot(_rms(h, gp_ref[...]).astype(BF16), wpg_ref[...]) + bpg_ref[...])
    h = h + gate * _dot(p_ref[...].astype(BF16), wple_ref[...])
    if final:
        h = _rms(h, gfin_ref[...])
    o_ref[...] = h


def _ffn(h, p, tails, w, *, tseq, final):
    m = h.shape[0]
    n_seq = m // tseq
    taps = CONV_W - 1
    tm = _tile(min(m, tseq) if tseq >= 256 else m, 256)
    row = lambda width: pl.BlockSpec((tm, width), lambda i: (i, 0))
    single = lambda shape: pl.BlockSpec(shape, lambda *_: (0,) * len(shape), pipeline_mode=pl.Buffered(1))
    if tseq >= tm:
        nt = tseq // tm
        tail_specs = [pl.BlockSpec((1, taps, D_FF), lambda i: (i // nt, 0, 0))]
        buf_spec = pl.BlockSpec((1, taps, D_FF), lambda i: (i // nt, 0, 0))
    else:
        assert tseq >= taps
        tail_specs = [row(D_FF)] * taps
        buf_spec = pl.BlockSpec((tm // tseq, taps, D_FF), lambda i: (i, 0, 0))
    vec = _const_spec((1, D_MODEL))
    return pl.pallas_call(
        functools.partial(_ffn_kernel, tseq=tseq, final=final, n_tails=len(tail_specs)),
        grid=(m // tm,),
        in_specs=[row(D_MODEL), row(PLE_DIM)] + tail_specs + [
            vec, single((D_MODEL, 2 * D_FF)), _const_spec((CONV_W, D_FF)), _const_spec((1, D_FF)),
            single((D_FF, D_MODEL)), vec, single((D_MODEL, D_MODEL)), vec, single((PLE_DIM, D_MODEL)), vec],
        out_specs=[row(D_MODEL), buf_spec],
        out_shape=[jax.ShapeDtypeStruct((m, D_MODEL), F32), jax.ShapeDtypeStruct((n_seq, taps, D_FF), F32)],
        scratch_shapes=[pltpu.VMEM((taps, D_FF), F32)],
        compiler_params=_cp("arbitrary"), name="conv_ffn_ple",
    )(h, p, *tails, w["norm_ffn"], w["w_up"], w["conv_w"], w["conv_b"], w["w_down"], w["norm_ple"],
      w["w_pg"], w["b_pg"], w["w_ple"], w["norm_f"])


def _block_diag(blocks):
    n, r, c = blocks.shape
    eye = jnp.eye(n, dtype=blocks.dtype)
    return (eye[:, None, :, None] * blocks[:, :, None, :]).reshape(n * r, n * c)


def _pad_cols(x, width):
    return jnp.pad(x, [(0, 0)] * (x.ndim - 1) + [(0, width - x.shape[-1])])


def _rope_tables(pos, reps):
    half = ROPE_D // 2
    inv = ROPE_THETA ** (-jnp.arange(half, dtype=F32) / half)
    ang = pos.astype(F32)[:, None] * inv[None, :]
    cos = jnp.cos(ang)
    sin = jnp.sin(ang)
    cos = jnp.concatenate([cos, cos], axis=-1)
    sin = jnp.concatenate([-sin, sin], axis=-1)
    return jnp.tile(cos, (1, reps)), jnp.tile(sin, (1, reps))


def _swap_halves_cols(w, group):
    shp = w.shape
    w = w.reshape(shp[:-1] + (shp[-1] // group, 2, group // 2))
    return w[..., ::-1, :].reshape(shp)


def _even_weights(W, j):
    w = W["w_in_e"][j]
    o = 3 * A_W
    w_r = jnp.concatenate([w[:, :o], w[:, o + H_A:], _pad_cols(w[:, o:o + H_A], LANE)], axis=1).astype(BF16)
    ab, bbr, bbi = _s5_prep(
        W["s5_lam_re"][j].reshape(1, S5_W), W["s5_lam_im"][j].reshape(1, S5_W),
        jnp.repeat(W["s5_log_dt"][j], S5_N).reshape(1, S5_W),
        W["s5_b_re"][j].reshape(S5_W, S5_P).T, W["s5_b_im"][j].reshape(S5_W, S5_P).T)
    to_bd = lambda t: _block_diag(t.T.reshape(S5_GROUPS, S5_N, S5_P).transpose(0, 2, 1)).astype(BF16)
    to_cd = lambda c: _block_diag(c.transpose(0, 2, 1)).astype(BF16)
    return dict(
        w_in=w_r, b_f=_pad_cols(W["b_f"][j].reshape(1, H_A), LANE), ab=ab, bd_re=to_bd(bbr), bd_im=to_bd(bbi),
        cd_re=to_cd(W["s5_c_re"][j]), cd_im=to_cd(W["s5_c_im"][j]), d=W["s5_d"][j].reshape(1, B_W),
        w_glu=W["w_glu"][j].astype(BF16), b_glu=W["b_glu"][j].reshape(1, B_W),
        w_out=W["w_out_e"][j].astype(BF16))


def _pad_c(x):
    o = 3 * C_W
    return jnp.concatenate([x[..., :o], _pad_cols(x[..., o:o + W_LORA], LANE),
                            _pad_cols(x[..., o + W_LORA:o + W_LORA + A_LORA], LANE),
                            _pad_cols(x[..., o + W_LORA + A_LORA:], LANE)], axis=-1)


def _unpad_c(x):
    o = 3 * C_W
    return jnp.concatenate([x[..., :o], x[..., o:o + W_LORA], x[..., o + LANE:o + LANE + A_LORA],
                            x[..., o + 2 * LANE:o + 2 * LANE + G_LORA]], axis=-1)


def _pad_rows(x, rows):
    return jnp.pad(x, [(0, rows - x.shape[0]), (0, 0)])


def _odd_weights(W, j):
    w = W["w_in_o"][j]
    wd = w[:, C_IN:]
    o = Q_LORA + KV_LORA
    kpe_w = wd[:, o:o + ROPE_D]
    w_d = _pad_cols(jnp.concatenate([wd[:, :o], kpe_w, _swap_halves_cols(kpe_w, ROPE_D)], axis=1), D_PAD)
    w_r = jnp.concatenate([_pad_c(w[:, :C_IN]), w_d], axis=1).astype(BF16)
    wq = W["w_q_up"][j]
    wq_pe = wq[..., NOPE_D:].reshape(Q_LORA, H_D * ROPE_D)
    wq_r = jnp.concatenate([wq[..., :NOPE_D].reshape(Q_LORA, H_D * NOPE_D), wq_pe,
                            _swap_halves_cols(wq_pe, ROPE_D)], axis=1).astype(BF16)
    wkv = W["w_kv_up"][j]
    bdk = _block_diag(wkv[..., :NOPE_D].transpose(1, 2, 0)).astype(BF16)
    bdv = _block_diag(wkv[..., NOPE_D:].transpose(1, 0, 2)).astype(BF16)
    vec = lambda x: x.reshape(1, -1)
    rw = dict(mu=vec(_pad_c(W["rw_mu"][j])), w0=vec(W["rw_w0"][j]),
              w2=_pad_rows(W["rw_w2"][j], LANE).astype(BF16), a0=vec(W["rw_a0"][j]),
              a2=_pad_rows(W["rw_a2"][j], LANE).astype(BF16), g2=_pad_rows(W["rw_g2"][j], LANE).astype(BF16),
              kk=vec(W["rw_kk"][j]), ka=vec(W["rw_ka"][j]), rk=vec(W["rw_rk"][j]))
    mla = dict(q_norm=vec(W["mla_q_norm"][j]), kv_norm=vec(W["mla_kv_norm"][j]), wq=wq_r, bdk=bdk)
    return dict(w_in=w_r, rw=rw, mla=mla, lnw=vec(W["rw_lnw"][j]), lnb=vec(W["rw_lnb"][j]), bdv=bdv,
                w_out=W["w_out_o"][j].astype(BF16))


def _ffn_weights(W, i):
    vec = lambda x: x.reshape(1, -1)
    return dict(norm_ffn=vec(W["norm_ffn"][i]), w_up=W["w_ffn_up"][i].astype(BF16), conv_w=W["ffn_conv_w"][i],
                conv_b=vec(W["ffn_conv_b"][i]), w_down=W["w_ffn_down"][i].astype(BF16),
                norm_ple=vec(W["norm_ple"][i]), w_pg=W["w_pg"][i].astype(BF16), b_pg=vec(W["b_pg"][i]),
                w_ple=W["w_ple"][i].astype(BF16), norm_f=vec(W["norm_f"]))


def _fox_prompt(q, k, lf, v, n_seq, tseq):
    _, hi, mid, lo = _cumsum(lf.reshape(n_seq, tseq, LANE), tseq)
    heads = lambda x: x.reshape(n_seq, tseq, H_A, DH_A)
    col = lambda x: x[..., :H_A, None]
    one = jnp.ones((n_seq, tseq, H_A, 3), BF16)
    zero = jnp.zeros((n_seq, tseq, H_A, LANE - DH_A - 6), BF16)
    qa = jnp.concatenate([heads(q), col(hi), col(mid), col(lo), one, zero], axis=-1)
    ka = jnp.concatenate([heads(k.astype(BF16)), one, col(-hi), col(-mid), col(-lo), zero], axis=-1)
    to_hm = lambda x: x.transpose(0, 2, 1, 3).reshape(n_seq * H_A, tseq, x.shape[-1])
    o = _flash(to_hm(qa), to_hm(ka), to_hm(heads(v.astype(BF16))), rpt=1, tq=512, tk=512, scale=1.0)
    return o.reshape(n_seq, H_A, tseq, DH_A).transpose(0, 2, 1, 3).reshape(n_seq * tseq, A_W)


def _fox_decode(j, q, k, lf, v, n_seq, tseq, caches, page_table):
    cache_k, cache_v, cache_lf_t = caches
    cn = _cumsum(lf.reshape(1, n_seq * tseq, LANE), tseq)[0].reshape(n_seq, tseq, LANE)[..., :H_A]
    eye = jnp.eye(H_A, dtype=BF16)
    qh = q.reshape(n_seq, tseq, H_A, DH_A)
    q_bd = (qh[:, :, :, None, :] * eye[None, None, :, :, None]).reshape(n_seq, tseq * H_A, A_W)
    cn_rows = jnp.broadcast_to(cn.reshape(n_seq, tseq * H_A, 1), (n_seq, tseq * H_A, LANE))
    cn_keys = _pad_cols(cn.transpose(0, 2, 1), LANE)
    pad_new = lambda x: jnp.pad(x.astype(BF16).reshape(n_seq, tseq, A_W), ((0, 0), (0, LANE - tseq), (0, 0)))
    o = _fox_sample(j, page_table, q_bd, pad_new(k), pad_new(v), cn_rows, cn_keys, cache_k, cache_v, cache_lf_t)
    return o.reshape(n_seq * tseq, A_W)


def _even_layer(h, ew, j, n_seq, tseq, s5_re, s5_im, fox_ctx):
    q, k, v, u, lf = _even_in(h, ew["norm_mix"], ew["w_in"], ew["b_f"])
    if fox_ctx is None:
        o_a = _fox_prompt(q, k, lf, v, n_seq, tseq)
    else:
        o_a = _fox_decode(j, q, k, lf, v, n_seq, tseq, *fox_ctx)
    y_b, n_re, n_im = _s5(u, ew["ab"], ew["bd_re"], ew["bd_im"], ew["cd_re"], ew["cd_im"], ew["d"],
                          ew["w_glu"], ew["b_glu"], s5_re.reshape(n_seq, S5_W), s5_im.reshape(n_seq, S5_W),
                          n_seq=n_seq, tseq=tseq)
    h = _even_out(h, o_a, y_b, ew["w_out"])
    rows = (k.reshape(n_seq, tseq, H_A, DH_A), v.reshape(n_seq, tseq, H_A, DH_A),
            lf[:, :H_A].reshape(n_seq, tseq, H_A))
    st = (n_re.reshape(n_seq, S5_GROUPS, S5_N), n_im.reshape(n_seq, S5_GROUPS, S5_N))
    return h, rows, st


def _odd_layer(h, ow, j, n_seq, tseq, pos, rw_state, rw_shift, mla_ctx):
    m = h.shape[0]
    z_c, z_d = _odd_in(h, ow["norm_mix"], ow["w_in"])
    shift_pad = _pad_c(rw_shift.astype(F32))
    if tseq >= 256:
        tails = shift_pad.reshape(n_seq, 1, C_PAD)
    else:
        tails = jnp.repeat(shift_pad, tseq, axis=0)
    r, w, k, v, kk, kka, g, bonus = _rwkv_pre(z_c, tails, ow["rw"], tseq=tseq)
    y, s_new = _rwkv_scan(r, w, k, v, kk, kka, rw_state.astype(F32), tseq=tseq)
    o_c = _rwkv_post(y, bonus, g, ow["lnw"], ow["lnb"])
    sh = _unpad_c(z_c.reshape(n_seq, tseq, C_PAD)[:, -1])

    cos_q, sin_q = _rope_tables(pos, H_D)
    cos_k, sin_k = _rope_tables(pos, 1)
    q_abs, q_pe, c, kpe = _mla_pre(z_d, ow["mla"], cos_q, sin_q, cos_k, sin_k, tseq=tseq)
    if mla_ctx is None:
        rows = tseq * H_D
        qq = jnp.concatenate([q_abs.reshape(n_seq, rows, KV_LORA), q_pe.reshape(n_seq, rows, ROPE_D),
                              jnp.zeros((n_seq, rows, QK_PAD - KV_LORA - ROPE_D), BF16)], axis=-1)
        cb = c.astype(BF16).reshape(n_seq, tseq, KV_LORA)
        kq = jnp.concatenate([cb, kpe.astype(BF16).reshape(n_seq, tseq, ROPE_D),
                              jnp.zeros((n_seq, tseq, QK_PAD - KV_LORA - ROPE_D), BF16)], axis=-1)
        lat = _flash(qq, kq, cb, rpt=H_D, tq=128, tk=512, scale=(NOPE_D + ROPE_D) ** -0.5)
    else:
        cache_ckv, cache_kpe, page_table = mla_ctx
        rows = tseq * H_D
        pad_new = lambda x, wdt: jnp.pad(x.astype(BF16).reshape(n_seq, tseq, wdt),
                                         ((0, 0), (0, LANE - tseq), (0, 0)))
        lat = _mla_sample(j, page_table, q_abs.reshape(n_seq, rows, KV_LORA), q_pe.reshape(n_seq, rows, ROPE_D),
                          pad_new(c, KV_LORA), pad_new(kpe, ROPE_D), cache_ckv, cache_kpe)
    h = _odd_out(h, o_c, lat.reshape(m, H_D * KV_LORA), ow["bdv"], ow["w_out"])
    return h, (s_new, sh), (c.reshape(n_seq, tseq, KV_LORA), kpe.reshape(n_seq, tseq, ROPE_D))


def _ffn_layer(h, p, buf, fw, n_seq, tseq, final):
    taps = CONV_W - 1
    if tseq >= 256:
        tails = [buf.astype(F32)]
    else:
        tails = [jnp.repeat(buf[:, i].astype(F32), tseq, axis=0) for i in range(taps)]
    return _ffn(h, p, tails, fw, tseq=tseq, final=final)


def _trunk(x, p, pos, s5_re, s5_im, rw_state, rw_shift, ffn_buf, weights, fox_ctx, mla_ctx):
    n_seq, tseq, _ = x.shape
    depth = p.shape[0]
    m = n_seq * tseq
    h = x.reshape(m, D_MODEL)
    fox_rows, s5_states, rw_states, mla_rows, ffn_bufs = [], [], [], [], []
    for i in range(depth):
        j = i // 2
        lw = weights["mix"][i]
        if i % 2 == 0:
            h, rows, st = _even_layer(h, lw, j, n_seq, tseq, s5_re[j], s5_im[j], fox_ctx)
            fox_rows.append(rows)
            s5_states.append(st)
        else:
            h, st, rows = _odd_layer(h, lw, j, n_seq, tseq, pos, rw_state[j], rw_shift[j], mla_ctx)
            rw_states.append(st)
            mla_rows.append(rows)
        h, nb = _ffn_layer(h, p[i].reshape(m, PLE_DIM), ffn_buf[i], weights["ffn"][i], n_seq, tseq,
                           final=(i == depth - 1))
        ffn_bufs.append(nb)
    stk = lambda xs, n: jnp.stack([r[n] for r in xs])
    return (h.reshape(n_seq, tseq, D_MODEL), stk(fox_rows, 0), stk(fox_rows, 1), stk(fox_rows, 2),
            stk(s5_states, 0), stk(s5_states, 1), stk(rw_states, 0), stk(rw_states, 1),
            stk(mla_rows, 0), stk(mla_rows, 1), jnp.stack(ffn_bufs))


def kernel(x_prompt, x_sample, cache_fox_k, cache_fox_v, cache_fox_lf, state_s5_re, state_s5_im, state_rwkv, state_shift, cache_mla_ckv, cache_mla_kpe, state_ffn_conv, page_table, p_prompt, p_sample, norm_mix, w_in_e, b_f, s5_lam_re, s5_lam_im, s5_log_dt, s5_b_re, s5_b_im, s5_c_re, s5_c_im, s5_d, w_glu, b_glu, w_out_e, w_in_o, rw_mu, rw_w0, rw_w2, rw_a0, rw_a2, rw_g2, rw_kk, rw_ka, rw_rk, rw_lnw, rw_lnb, mla_q_norm, w_q_up, mla_kv_norm, w_kv_up, w_out_o, norm_ffn, w_ffn_up, ffn_conv_w, ffn_conv_b, w_ffn_down, norm_ple, w_pg, b_pg, w_ple, norm_f):
    W = dict(w_in_e=w_in_e, b_f=b_f, s5_lam_re=s5_lam_re, s5_lam_im=s5_lam_im, s5_log_dt=s5_log_dt,
             s5_b_re=s5_b_re, s5_b_im=s5_b_im, s5_c_re=s5_c_re, s5_c_im=s5_c_im, s5_d=s5_d, w_glu=w_glu,
             b_glu=b_glu, w_out_e=w_out_e, w_in_o=w_in_o, rw_mu=rw_mu, rw_w0=rw_w0, rw_w2=rw_w2, rw_a0=rw_a0,
             rw_a2=rw_a2, rw_g2=rw_g2, rw_kk=rw_kk, rw_ka=rw_ka, rw_rk=rw_rk, rw_lnw=rw_lnw, rw_lnb=rw_lnb,
             mla_q_norm=mla_q_norm, w_q_up=w_q_up, mla_kv_norm=mla_kv_norm, w_kv_up=w_kv_up, w_out_o=w_out_o,
             norm_ffn=norm_ffn, w_ffn_up=w_ffn_up, ffn_conv_w=ffn_conv_w, ffn_conv_b=ffn_conv_b,
             w_ffn_down=w_ffn_down, norm_ple=norm_ple, w_pg=w_pg, b_pg=b_pg, w_ple=w_ple, norm_f=norm_f)
    depth = p_prompt.shape[0]
    ne = (depth + 1) // 2
    no = depth // 2
    mix = []
    for i in range(depth):
        lw = _even_weights(W, i // 2) if i % 2 == 0 else _odd_weights(W, i // 2)
        lw["norm_mix"] = norm_mix[i].reshape(1, D_MODEL)
        mix.append(lw)
    weights = dict(mix=mix, ffn=[_ffn_weights(W, i) for i in range(depth)])

    bp, sp, _ = x_prompt.shape
    zeros = lambda *shape: jnp.zeros(shape, F32)
    out_p = _trunk(x_prompt, p_prompt, jnp.arange(sp), zeros(ne, bp, S5_GROUPS, S5_N),
                   zeros(ne, bp, S5_GROUPS, S5_N), zeros(no, bp, H_C, DH_C, DH_C), zeros(no, bp, C_IN),
                   zeros(depth, bp, CONV_W - 1, D_FF), weights, None, None)

    n_pool, page = cache_fox_k.shape[1], cache_fox_k.shape[2]
    past = page_table.shape[1] * page
    fox_ctx = ((cache_fox_k.reshape(ne, n_pool, page, A_W), cache_fox_v.reshape(ne, n_pool, page, A_W),
                cache_fox_lf.astype(F32).transpose(0, 1, 3, 2)), page_table)
    mla_ctx = (cache_mla_ckv, cache_mla_kpe, page_table)
    out_s = _trunk(x_sample, p_sample, past + jnp.arange(x_sample.shape[1]), state_s5_re, state_s5_im,
                   state_rwkv, state_shift, state_ffn_conv, weights, fox_ctx, mla_ctx)

    (y_p, fk_p, fv_p, fl_p, sr_p, si_p, rw_p, sh_p, ck_p, kp_p, ff_p) = out_p
    (y_s, fk_s, fv_s, fl_s, sr_s, si_s, rw_s, sh_s, ck_s, kp_s, ff_s) = out_s
    return (y_p, y_s, fk_p, fv_p, fl_p, fk_s, fv_s, fl_s, sr_p, si_p, sr_s, si_s, rw_p, sh_p, rw_s, sh_s,
            ck_p, kp_p, ck_s, kp_s, ff_p, ff_s)
```

```python
import functools
import math

import jax
import jax.numpy as jnp
from jax import lax
from jax.experimental import pallas as pl
from jax.experimental.pallas import tpu as pltpu

F32 = jnp.float32
BF16 = jnp.bfloat16

D_MODEL = 1024
H_A = 8
DH_A = 64
A_W = H_A * DH_A
S5_GROUPS = 32
S5_P = 16
S5_N = 64
B_W = S5_GROUPS * S5_P
S5_W = S5_GROUPS * S5_N
H_C = 8
DH_C = 64
C_W = H_C * DH_C
W_LORA = 32
A_LORA = 32
G_LORA = 96
C_IN = 3 * C_W + W_LORA + A_LORA + G_LORA
RWKV_GN_EPS = 64e-5
H_D = 8
Q_LORA = 256
KV_LORA = 128
NOPE_D = 64
ROPE_D = 32
V_D = 64
ROPE_THETA = 10000.0
D_FF = 2816
CONV_W = 3
PLE_DIM = 256
NORM_EPS = 1e-6
NEG = -1e30

LANE = 128
SUBLANE = 8
VMEM_LIMIT = 56 * 1024 * 1024
C_PAD = 3 * C_W + 3 * LANE
D_PAD = 512
QK_PAD = 256
FOX_PAGES_PER_STEP = 16
MLA_PAGES_PER_STEP = 32
S5_SCAN_LANES = 512
RW_CHUNK = 32
RW_SUPER = 256


def _cp(*sem):
    return pltpu.CompilerParams(dimension_semantics=sem, vmem_limit_bytes=VMEM_LIMIT)


def _tile(n, pref):
    if n <= pref:
        return n
    t = pref - pref % SUBLANE
    while t >= SUBLANE:
        if n % t == 0:
            return t
        t -= SUBLANE
    return n


def _const_spec(shape):
    nd = len(shape)
    return pl.BlockSpec(shape, lambda *_: (0,) * nd)


def _rms(x, g):
    return x * lax.rsqrt(jnp.mean(x * x, axis=-1, keepdims=True) + NORM_EPS) * g


def _split3(x):
    hi = x.astype(BF16)
    r1 = x - hi.astype(F32)
    mid = r1.astype(BF16)
    lo = (r1 - mid.astype(F32)).astype(BF16)
    return hi, mid, lo


def _dot(a, b):
    return jnp.dot(a, b, preferred_element_type=F32)


def _dot3(x, m):
    hi, mid, lo = _split3(x)
    return _dot(hi, m) + _dot(mid, m) + _dot(lo, m)


def _dot3_left(m, x):
    hi, mid, lo = _split3(x)
    return _dot(m, hi) + _dot(m, mid) + _dot(m, lo)


def _log_sigmoid(x):
    return jnp.minimum(x, 0.0) - jnp.log1p(jnp.exp(-jnp.abs(x)))


def _sigmoid(x):
    return 1.0 / (1.0 + jnp.exp(-x))


def _softplus(x):
    return jnp.maximum(x, 0.0) + jnp.log1p(jnp.exp(-jnp.abs(x)))


def _gelu(x):
    return jax.nn.gelu(x, approximate=True)


def _head_ones(width, head):
    r = lax.broadcasted_iota(jnp.int32, (width, width), 0) // head
    c = lax.broadcasted_iota(jnp.int32, (width, width), 1) // head
    return (r == c).astype(BF16)


def _even_in_kernel(h_ref, g_ref, w_ref, bf_ref, q_ref, k_ref, v_ref, u_ref, lf_ref):
    xn = _rms(h_ref[...], g_ref[...]).astype(BF16)
    z = _dot(xn, w_ref[...])
    q_ref[...] = (z[:, :A_W] * (DH_A ** -0.5)).astype(BF16)
    k_ref[...] = z[:, A_W:2 * A_W]
    v_ref[...] = z[:, 2 * A_W:3 * A_W]
    u_ref[...] = z[:, 3 * A_W:3 * A_W + B_W]
    lf_ref[...] = _log_sigmoid(z[:, 3 * A_W + B_W:] + bf_ref[...])


def _even_in(h, g, w, bf):
    m = h.shape[0]
    tm = _tile(m, 512)
    n = w.shape[1]
    row = lambda width: pl.BlockSpec((tm, width), lambda i: (i, 0))
    return pl.pallas_call(
        _even_in_kernel,
        grid=(m // tm,),
        in_specs=[row(D_MODEL), _const_spec((1, D_MODEL)), _const_spec((D_MODEL, n)), _const_spec((1, LANE))],
        out_specs=[row(A_W), row(A_W), row(A_W), row(B_W), row(LANE)],
        out_shape=[jax.ShapeDtypeStruct((m, A_W), BF16), jax.ShapeDtypeStruct((m, A_W), F32),
                   jax.ShapeDtypeStruct((m, A_W), F32), jax.ShapeDtypeStruct((m, B_W), F32),
                   jax.ShapeDtypeStruct((m, LANE), F32)],
        compiler_params=_cp("parallel"),
        name="even_in",
    )(h, g, w, bf)


def _cumsum_kernel(x_ref, c_ref, hi_ref, mid_ref, lo_ref, carry_ref, *, tseq):
    tc = x_ref.shape[1]
    ri = lax.broadcasted_iota(jnp.int32, (tc, tc), 0)
    ci = lax.broadcasted_iota(jnp.int32, (tc, tc), 1)
    if tseq >= tc:
        tri = (ri >= ci).astype(BF16)

        @pl.when((pl.program_id(1) * tc) % tseq == 0)
        def _():
            carry_ref[...] = jnp.zeros_like(carry_ref)
    else:
        tri = ((ri >= ci) & (ri // tseq == ci // tseq)).astype(BF16)
    c = _dot3_left(tri, x_ref[0])
    if tseq >= tc:
        c = c + carry_ref[...]
        carry_ref[...] = c[tc - 1:tc, :]
    c_ref[0] = c
    hi, mid, lo = _split3(c)
    hi_ref[0] = hi
    mid_ref[0] = mid
    lo_ref[0] = lo


def _cumsum(x, tseq):
    g, length, _ = x.shape
    tc = _tile(length, 512)
    assert tseq % tc == 0 or tc % tseq == 0
    blk = pl.BlockSpec((1, tc, LANE), lambda i, j: (i, j, 0))
    return pl.pallas_call(
        functools.partial(_cumsum_kernel, tseq=tseq),
        grid=(g, length // tc),
        in_specs=[blk],
        out_specs=[blk] * 4,
        out_shape=[jax.ShapeDtypeStruct(x.shape, F32)] + [jax.ShapeDtypeStruct(x.shape, BF16)] * 3,
        scratch_shapes=[pltpu.VMEM((1, LANE), F32)],
        compiler_params=_cp("arbitrary", "arbitrary"),
        name="fox_cumsum",
    )(x)


def _flash_kernel(q_ref, k_ref, v_ref, o_ref, *, rpt, tk, scale, dv):
    rows = q_ref.shape[1]
    tq = rows // rpt
    t0 = pl.program_id(1) * tq
    q = q_ref[0]
    if scale == 1.0:
        factor, ex = 1.0, jnp.exp
    else:
        factor, ex = scale * math.log2(math.e), jnp.exp2

    ones_col = v_ref.shape[2] > dv

    def chunk(kc, carry, masked):
        m, l, acc = carry
        k0 = pl.multiple_of(kc * tk, tk)
        kb = k_ref[0, pl.ds(k0, tk), :]
        vb = v_ref[0, pl.ds(k0, tk), :]
        s = lax.dot_general(q, kb, _NT, preferred_element_type=F32)
        if masked:
            tok = t0 + lax.broadcasted_iota(jnp.int32, (rows, tk), 0) // rpt
            key = k0 + lax.broadcasted_iota(jnp.int32, (rows, tk), 1)
            s = jnp.where(key <= tok, s, NEG)
        if factor != 1.0:
            s = s * factor
        m_new = jnp.maximum(m, jnp.max(s, axis=1, keepdims=True))
        p = ex(s - m_new)
        alpha = ex(m - m_new)
        if not ones_col:
            l = alpha * l + jnp.sum(p, axis=1, keepdims=True)
        acc = alpha * acc + _dot(p.astype(BF16), vb)
        return m_new, l, acc

    init = (jnp.full((rows, 1), NEG, F32), jnp.zeros((rows, 1), F32), jnp.zeros((rows, v_ref.shape[2]), F32))
    n_full = t0 // tk
    carry = lax.fori_loop(0, n_full, lambda kc, c: chunk(kc, c, False), init)
    for j in range(max(1, tq // tk)):
        carry = chunk(n_full + j, carry, True)
    _, l, acc = carry
    if ones_col:
        l = acc[:, dv:dv + 1]
    o_ref[0] = (acc[:, :dv] / l).astype(o_ref.dtype)


def _flash(q, k, v, *, rpt, tq, tk, scale):
    g, rows_total, dq = q.shape
    s = k.shape[1]
    dv = v.shape[2]
    tq = min(tq, s)
    tk = min(tk, s)
    assert s % tq == 0 and s % tk == 0 and (tk % tq == 0 or tq % tk == 0)
    rows = tq * rpt
    if dv % LANE:
        dva = dv + LANE - dv % LANE
        v_aug = jnp.concatenate([v, jnp.ones((g, s, 1), BF16), jnp.zeros((g, s, dva - dv - 1), BF16)], axis=-1)
    else:
        dva, v_aug = dv, v
    return pl.pallas_call(
        functools.partial(_flash_kernel, rpt=rpt, tk=tk, scale=scale, dv=dv),
        grid=(g, s // tq),
        in_specs=[pl.BlockSpec((1, rows, dq), lambda i, j: (i, j, 0)),
                  pl.BlockSpec((1, s, dq), lambda i, j: (i, 0, 0)),
                  pl.BlockSpec((1, s, dva), lambda i, j: (i, 0, 0))],
        out_specs=pl.BlockSpec((1, rows, dv), lambda i, j: (i, j, 0)),
        out_shape=jax.ShapeDtypeStruct((g, rows_total, dv), BF16),
        compiler_params=_cp("parallel", "arbitrary"),
        name="flash_prompt",
    )(q, k, v_aug)


_NT = (((1,), (1,)), ((), ()))


def _online_update(s, vals, m_ref, l_ref, acc_ref, *, vals_transposed=False):
    m_old = m_ref[...]
    m_new = m_old
    for blk in s:
        m_new = jnp.maximum(m_new, jnp.max(blk, axis=1, keepdims=True))
    alpha = jnp.exp(m_old - m_new)
    l = alpha * l_ref[...]
    acc = alpha * acc_ref[...]
    for blk, val in zip(s, vals):
        p = jnp.exp(blk - m_new)
        l = l + jnp.sum(p, axis=1, keepdims=True)
        if vals_transposed:
            acc = acc + lax.dot_general(p.astype(BF16), val, _NT, preferred_element_type=F32)
        else:
            acc = acc + _dot(p.astype(BF16), val)
    m_ref[...] = m_new
    l_ref[...] = l
    acc_ref[...] = acc


def _new_key_mask(rows, t_new):
    tok = lax.broadcasted_iota(jnp.int32, (rows, LANE), 0) // (rows // t_new)
    key = lax.broadcasted_iota(jnp.int32, (rows, LANE), 1)
    return key <= tok


def _fox_sample_kernel(pt_ref, q_ref, kn_ref, vn_ref, cnr_ref, cnk_ref, *rest, npg, t_new):
    k_refs = rest[:npg]
    v_refs = rest[npg:2 * npg]
    lf_refs = rest[2 * npg:3 * npg]
    o_ref, m_ref, l_ref, acc_ref, carry_ref = rest[3 * npg:]
    c = pl.program_id(1)
    rows = q_ref.shape[1]
    q = q_ref[0]
    cn_rows = cnr_ref[0]

    @pl.when(c == 0)
    def _():
        m_ref[...] = jnp.full_like(m_ref, NEG)
        l_ref[...] = jnp.zeros_like(l_ref)
        acc_ref[...] = jnp.zeros_like(acc_ref)
        carry_ref[...] = jnp.zeros_like(carry_ref)
        s = _dot(q, kn_ref[0])
        s = s + cn_rows - jnp.tile(cnk_ref[0], (t_new, 1))
        s = jnp.where(_new_key_mask(rows, t_new), s, NEG)
        _online_update([s], [vn_ref[0]], m_ref, l_ref, acc_ref, vals_transposed=True)

    ri = lax.broadcasted_iota(jnp.int32, (LANE, LANE), 0)
    ci = lax.broadcasted_iota(jnp.int32, (LANE, LANE), 1)
    after = (ri > ci).astype(BF16)
    ones = jnp.ones((LANE, LANE), BF16)
    lf_all = jnp.concatenate([r[...] for r in lf_refs], axis=0)
    suffix = _dot3(lf_all, after)
    total = _dot3(lf_all, ones)
    carry = carry_ref[...]
    scores = [None] * npg
    for i in reversed(range(npg)):
        s = _dot(q, k_refs[i][...].astype(BF16))
        bias = suffix[i * H_A:(i + 1) * H_A] + carry
        scores[i] = s + cn_rows + jnp.tile(bias, (t_new, 1))
        carry = carry + total[i * H_A:(i + 1) * H_A]
    carry_ref[...] = carry
    _online_update(scores, [r[...].astype(BF16) for r in v_refs], m_ref, l_ref, acc_ref, vals_transposed=True)

    @pl.when(c == pl.num_programs(1) - 1)
    def _():
        out = acc_ref[...] / l_ref[...]
        rh = lax.broadcasted_iota(jnp.int32, out.shape, 0) % H_A
        ch = lax.broadcasted_iota(jnp.int32, out.shape, 1) // DH_A
        out = jnp.where(rh == ch, out, 0.0).reshape(t_new, H_A, A_W)
        o_ref[0] = jnp.sum(out, axis=1).astype(o_ref.dtype)


def _fox_sample(layer, page_table, q_bd, k_new_t, v_new_t, cn_rows, cn_keys, cache_k_t, cache_v_t, cache_lf_t):
    b, n_pages = page_table.shape
    page = cache_k_t.shape[3]
    npg = min(FOX_PAGES_PER_STEP, n_pages)
    assert page == LANE and n_pages % npg == 0
    n_chunks = n_pages // npg
    rows = q_bd.shape[1]
    t_new = rows // H_A

    def seq_spec(shape):
        return pl.BlockSpec((1,) + shape, lambda i, c, pt: (i, 0, 0))

    def page_spec(shape, idx):
        def imap(i, c, pt):
            return (layer, pt[i, (n_chunks - 1 - c) * npg + idx], 0, 0)
        return pl.BlockSpec((None, None) + shape, imap)

    in_specs = [seq_spec((rows, A_W)), seq_spec((A_W, LANE)), seq_spec((A_W, LANE)),
                seq_spec((rows, LANE)), seq_spec((H_A, LANE))]
    in_specs += [page_spec((A_W, page), i) for i in range(npg)]
    in_specs += [page_spec((A_W, page), i) for i in range(npg)]
    in_specs += [page_spec((H_A, page), i) for i in range(npg)]
    grid_spec = pltpu.PrefetchScalarGridSpec(
        num_scalar_prefetch=1, grid=(b, n_chunks), in_specs=in_specs,
        out_specs=pl.BlockSpec((1, t_new, A_W), lambda i, c, pt: (i, 0, 0)),
        scratch_shapes=[pltpu.VMEM((rows, 1), F32), pltpu.VMEM((rows, 1), F32),
                        pltpu.VMEM((rows, A_W), F32), pltpu.VMEM((H_A, LANE), F32)])
    return pl.pallas_call(
        functools.partial(_fox_sample_kernel, npg=npg, t_new=t_new),
        grid_spec=grid_spec,
        out_shape=jax.ShapeDtypeStruct((b, t_new, A_W), BF16),
        compiler_params=_cp("parallel", "arbitrary"),
        name="fox_sample",
    )(page_table, q_bd, k_new_t, v_new_t, cn_rows, cn_keys,
      *([cache_k_t] * npg), *([cache_v_t] * npg), *([cache_lf_t] * npg))


def _mla_sample_kernel(pt_ref, qa_ref, qp_ref, cn_ref, kn_ref, *rest, npg, t_new, scale):
    c_refs = rest[:npg]
    p_refs = rest[npg:2 * npg]
    o_ref, m_ref, l_ref, acc_ref = rest[2 * npg:]
    c = pl.program_id(1)
    rows = qa_ref.shape[1]
    qa = qa_ref[0]
    qp = qp_ref[0]

    def score(cb, pb_t):
        return (lax.dot_general(qa, cb, _NT, preferred_element_type=F32) + _dot(qp, pb_t)) * scale

    @pl.when(c == 0)
    def _():
        m_ref[...] = jnp.full_like(m_ref, NEG)
        l_ref[...] = jnp.zeros_like(l_ref)
        acc_ref[...] = jnp.zeros_like(acc_ref)
        s = jnp.where(_new_key_mask(rows, t_new), score(cn_ref[0], kn_ref[0]), NEG)
        _online_update([s], [cn_ref[0]], m_ref, l_ref, acc_ref)

    cbs = [r[...].astype(BF16) for r in c_refs]
    scores = [score(cb, r[...].astype(BF16)) for cb, r in zip(cbs, p_refs)]
    _online_update(scores, cbs, m_ref, l_ref, acc_ref)

    @pl.when(c == pl.num_programs(1) - 1)
    def _():
        o_ref[0] = (acc_ref[...] / l_ref[...]).astype(o_ref.dtype)


def _mla_sample(layer, page_table, q_abs, q_pe, c_new, kpe_new_t, cache_ckv, cache_kpe_t):
    b, n_pages = page_table.shape
    page = cache_ckv.shape[2]
    npg = min(MLA_PAGES_PER_STEP, n_pages)
    assert page == LANE and n_pages % npg == 0
    n_chunks = n_pages // npg
    rows = q_abs.shape[1]
    t_new = rows // H_D

    def seq_spec(shape):
        return pl.BlockSpec((1,) + shape, lambda i, c, pt: (i, 0, 0))

    def page_spec(shape, idx):
        def imap(i, c, pt):
            return (layer, pt[i, c * npg + idx], 0, 0)
        return pl.BlockSpec((None, None) + shape, imap)

    in_specs = [seq_spec((rows, KV_LORA)), seq_spec((rows, ROPE_D)),
                seq_spec((LANE, KV_LORA)), seq_spec((ROPE_D, LANE))]
    in_specs += [page_spec((page, KV_LORA), i) for i in range(npg)]
    in_specs += [page_spec((ROPE_D, page), i) for i in range(npg)]
    grid_spec = pltpu.PrefetchScalarGridSpec(
        num_scalar_prefetch=1, grid=(b, n_chunks), in_specs=in_specs,
        out_specs=pl.BlockSpec((1, rows, KV_LORA), lambda i, c, pt: (i, 0, 0)),
        scratch_shapes=[pltpu.VMEM((rows, 1), F32), pltpu.VMEM((rows, 1), F32),
                        pltpu.VMEM((rows, KV_LORA), F32)])
    return pl.pallas_call(
        functools.partial(_mla_sample_kernel, npg=npg, t_new=t_new, scale=(NOPE_D + ROPE_D) ** -0.5),
        grid_spec=grid_spec,
        out_shape=jax.ShapeDtypeStruct((b, rows, KV_LORA), BF16),
        compiler_params=_cp("parallel", "arbitrary"),
        name="mla_sample",
    )(page_table, q_abs, q_pe, c_new, kpe_new_t, *([cache_ckv] * npg), *([cache_kpe_t] * npg))


def _s5_prep_kernel(lr_ref, li_ref, ldt_ref, br_ref, bi_ref, ab_ref, bbr_ref, bbi_ref):
    lr = lr_ref[...]
    li = li_ref[...]
    dt = jnp.exp(ldt_ref[...])
    mag = jnp.exp(lr * dt)
    ang = li * dt
    ab_re = mag * jnp.cos(ang)
    ab_im = mag * jnp.sin(ang)
    den = lr * lr + li * li
    f_re = ((ab_re - 1.0) * lr + ab_im * li) / den
    f_im = (ab_im * lr - (ab_re - 1.0) * li) / den
    ab_ref[0:1, :] = ab_re
    ab_ref[1:2, :] = ab_im
    br = br_ref[...]
    bi = bi_ref[...]
    bbr_ref[...] = f_re * br - f_im * bi
    bbi_ref[...] = f_re * bi + f_im * br


def _s5_prep(lam_re, lam_im, log_dt, b_re_t, b_im_t):
    return pl.pallas_call(
        _s5_prep_kernel,
        out_shape=[jax.ShapeDtypeStruct((2, S5_W), F32), jax.ShapeDtypeStruct((S5_P, S5_W), F32),
                   jax.ShapeDtypeStruct((S5_P, S5_W), F32)],
        name="s5_prep",
    )(lam_re, lam_im, log_dt, b_re_t, b_im_t)


def _cmul(ar, ai, br, bi):
    return ar * br - ai * bi, ar * bi + ai * br


def _s5_kernel(u_ref, ab_ref, bdr_ref, bdi_ref, cdr_ref, cdi_ref, d_ref, wg_ref, bg_ref, h0r_ref, h0i_ref,
               y_ref, sr_ref, si_ref, xr_ref, xi_ref, car_ref, cai_ref, *, grouped):
    tc = u_ref.shape[0]
    u = u_ref[...]
    ub = u.astype(BF16)
    xr_ref[...] = _dot(ub, bdr_ref[...])
    xi_ref[...] = _dot(ub, bdi_ref[...])

    if not grouped:
        @pl.when(pl.program_id(1) == 0)
        def _():
            car_ref[...] = h0r_ref[0]
            cai_ref[...] = h0i_ref[0]

    cw = S5_SCAN_LANES
    tile = (SUBLANE, cw)
    row = lax.broadcasted_iota(jnp.int32, tile, 0)
    n_groups = tc // SUBLANE
    for j in range(S5_W // cw):
        lanes = slice(j * cw, (j + 1) * cw)
        a1r = jnp.broadcast_to(ab_ref[0:1, lanes], tile)
        a1i = jnp.broadcast_to(ab_ref[1:2, lanes], tile)
        a2r, a2i = _cmul(a1r, a1i, a1r, a1i)
        a4r, a4i = _cmul(a2r, a2i, a2r, a2i)
        apr, api = a1r, a1i
        for d, (pr, pi) in ((1, (a1r, a1i)), (2, (a2r, a2i)), (4, (a4r, a4i))):
            sr = pltpu.roll(apr, d, axis=0)
            si = pltpu.roll(api, d, axis=0)
            mr, mi = _cmul(pr, pi, sr, si)
            apr = jnp.where(row >= d, mr, apr)
            api = jnp.where(row >= d, mi, api)

        def body(g, carry):
            cr, ci = carry
            r0 = pl.multiple_of(g * SUBLANE, SUBLANE)
            xr = xr_ref[pl.ds(r0, SUBLANE), lanes]
            xi = xi_ref[pl.ds(r0, SUBLANE), lanes]
            for d, (pr, pi) in ((1, (a1r, a1i)), (2, (a2r, a2i)), (4, (a4r, a4i))):
                sr = jnp.where(row >= d, pltpu.roll(xr, d, axis=0), 0.0)
                si = jnp.where(row >= d, pltpu.roll(xi, d, axis=0), 0.0)
                mr, mi = _cmul(pr, pi, sr, si)
                xr = xr + mr
                xi = xi + mi
            if grouped:
                cr = jnp.broadcast_to(h0r_ref[pl.ds(g, 1), lanes], tile)
                ci = jnp.broadcast_to(h0i_ref[pl.ds(g, 1), lanes], tile)
            mr, mi = _cmul(apr, api, cr, ci)
            xr = xr + mr
            xi = xi + mi
            xr_ref[pl.ds(r0, SUBLANE), lanes] = xr
            xi_ref[pl.ds(r0, SUBLANE), lanes] = xi
            if grouped:
                sr_ref[pl.ds(g, 1), lanes] = xr[SUBLANE - 1:SUBLANE]
                si_ref[pl.ds(g, 1), lanes] = xi[SUBLANE - 1:SUBLANE]
                return cr, ci
            last_r = jnp.broadcast_to(xr[SUBLANE - 1:SUBLANE], tile)
            last_i = jnp.broadcast_to(xi[SUBLANE - 1:SUBLANE], tile)
            return last_r, last_i

        if grouped:
            init = (jnp.zeros(tile, F32), jnp.zeros(tile, F32))
        else:
            init = (jnp.broadcast_to(car_ref[0:1, lanes], tile), jnp.broadcast_to(cai_ref[0:1, lanes], tile))
        cr, ci = lax.fori_loop(0, n_groups, body, init)
        if not grouped:
            car_ref[0:1, lanes] = cr[0:1]
            cai_ref[0:1, lanes] = ci[0:1]
            sr_ref[0, 0:1, lanes] = cr[0:1]
            si_ref[0, 0:1, lanes] = ci[0:1]

    y = (_dot(xr_ref[...].astype(BF16), cdr_ref[...]) - _dot(xi_ref[...].astype(BF16), cdi_ref[...])
         + d_ref[...] * u)
    y = _gelu(y)
    y = y * _sigmoid(_dot(y.astype(BF16), wg_ref[...]) + bg_ref[...])
    y_ref[...] = y.astype(y_ref.dtype)


def _s5(u, ab, bd_re, bd_im, cd_re, cd_im, d, w_glu, b_glu, h0_re, h0_im, *, n_seq, tseq):
    m = u.shape[0]
    grouped = tseq == SUBLANE
    consts = [_const_spec((2, S5_W)), _const_spec((B_W, S5_W)), _const_spec((B_W, S5_W)),
              _const_spec((S5_W, B_W)), _const_spec((S5_W, B_W)), _const_spec((1, B_W)),
              _const_spec((B_W, B_W)), _const_spec((1, B_W))]
    if grouped:
        tc = _tile(m, 256)
        ng = tc // SUBLANE
        grid = (m // tc, 1)
        u_spec = pl.BlockSpec((tc, B_W), lambda i, j: (i, 0))
        st_spec = pl.BlockSpec((ng, S5_W), lambda i, j: (i, 0))
        h0_spec = st_spec
        st_shape = jax.ShapeDtypeStruct((n_seq, S5_W), F32)
    else:
        tc = _tile(tseq, 256)
        assert tc % SUBLANE == 0
        nt = tseq // tc
        grid = (n_seq, nt)
        u_spec = pl.BlockSpec((tc, B_W), lambda i, j: (i * nt + j, 0))
        st_spec = pl.BlockSpec((1, 1, S5_W), lambda i, j: (i, 0, 0))
        h0_spec = st_spec
        st_shape = jax.ShapeDtypeStruct((n_seq, 1, S5_W), F32)
        h0_re = h0_re.reshape(n_seq, 1, S5_W)
        h0_im = h0_im.reshape(n_seq, 1, S5_W)
    y, s_re, s_im = pl.pallas_call(
        functools.partial(_s5_kernel, grouped=grouped),
        grid=grid,
        in_specs=[u_spec] + consts + [h0_spec, h0_spec],
        out_specs=[u_spec, st_spec, st_spec],
        out_shape=[jax.ShapeDtypeStruct((m, B_W), BF16), st_shape, st_shape],
        scratch_shapes=[pltpu.VMEM((tc, S5_W), F32), pltpu.VMEM((tc, S5_W), F32),
                        pltpu.VMEM((1, S5_W), F32), pltpu.VMEM((1, S5_W), F32)],
        compiler_params=_cp("arbitrary", "arbitrary"),
        name="s5_mix",
    )(u, ab, bd_re, bd_im, cd_re, cd_im, d, w_glu, b_glu, h0_re, h0_im)
    return y, s_re.reshape(n_seq, S5_W), s_im.reshape(n_seq, S5_W)


def _even_out_kernel(h_ref, a_ref, b_ref, w_ref, o_ref):
    o_ref[...] = h_ref[...] + _dot(a_ref[...], w_ref[0:A_W, :]) + _dot(b_ref[...], w_ref[A_W:, :])


def _odd_out_kernel(h_ref, a_ref, lat_ref, bdv_ref, w_ref, o_ref):
    o_d = _dot(lat_ref[...], bdv_ref[...]).astype(BF16)
    o_ref[...] = h_ref[...] + _dot(a_ref[...], w_ref[0:C_W, :]) + _dot(o_d, w_ref[C_W:, :])


def _even_out(h, a, b, w):
    m = h.shape[0]
    tm = _tile(m, 512)
    row = lambda width: pl.BlockSpec((tm, width), lambda i: (i, 0))
    return pl.pallas_call(
        _even_out_kernel, grid=(m // tm,),
        in_specs=[row(D_MODEL), row(A_W), row(B_W), _const_spec(w.shape)],
        out_specs=row(D_MODEL), out_shape=jax.ShapeDtypeStruct((m, D_MODEL), F32),
        compiler_params=_cp("parallel"), name="even_out",
    )(h, a, b, w)


def _odd_out(h, a, lat, bdv, w):
    m = h.shape[0]
    tm = _tile(m, 512)
    row = lambda width: pl.BlockSpec((tm, width), lambda i: (i, 0))
    return pl.pallas_call(
        _odd_out_kernel, grid=(m // tm,),
        in_specs=[row(D_MODEL), row(C_W), row(H_D * KV_LORA), _const_spec(bdv.shape), _const_spec(w.shape)],
        out_specs=row(D_MODEL), out_shape=jax.ShapeDtypeStruct((m, D_MODEL), F32),
        compiler_params=_cp("parallel"), name="odd_out",
    )(h, a, lat, bdv, w)


def _shifted(x, d, tseq, tails):
    tm = x.shape[0]
    rolled = pltpu.roll(x, d, axis=0)
    row = lax.broadcasted_iota(jnp.int32, (tm, 1), 0)
    if tseq >= tm:
        taps = tails.shape[0]
        out = rolled
        for i in range(d):
            out = jnp.where(row == i, tails[taps - d + i:taps - d + i + 1, :], out)
        return out
    taps = len(tails)
    pos = row % tseq
    out = rolled
    for i in range(d):
        out = jnp.where(pos == i, tails[taps - d + i], out)
    return out


def _odd_in_kernel(h_ref, g_ref, w_ref, zc_ref, zd_ref):
    xn = _rms(h_ref[...], g_ref[...]).astype(BF16)
    z = _dot(xn, w_ref[...])
    zc_ref[...] = z[:, :C_PAD]
    zd_ref[...] = z[:, C_PAD:]


def _odd_in(h, g, w):
    m = h.shape[0]
    tm = _tile(m, 512)
    row = lambda width: pl.BlockSpec((tm, width), lambda i: (i, 0))
    return pl.pallas_call(
        _odd_in_kernel, grid=(m // tm,),
        in_specs=[row(D_MODEL), _const_spec((1, D_MODEL)), _const_spec(w.shape)],
        out_specs=[row(C_PAD), row(D_PAD)],
        out_shape=[jax.ShapeDtypeStruct((m, C_PAD), F32), jax.ShapeDtypeStruct((m, D_PAD), F32)],
        compiler_params=_cp("parallel"), name="odd_in",
    )(h, g, w)


def _rwkv_pre_kernel(z_ref, tail_ref, mu_ref, w0_ref, w2_ref, a0_ref, a2_ref, g2_ref, kkw_ref, kaw_ref,
                     rk_ref, r_ref, w_ref, k_ref, v_ref, kk_ref, kka_ref, g_ref, bonus_ref, carry_ref,
                     *, tseq):
    tm = z_ref.shape[0]
    z = z_ref[...]
    if tseq >= tm:
        @pl.when((pl.program_id(0) * tm) % tseq == 0)
        def _():
            carry_ref[...] = tail_ref[0]
        zprev = _shifted(z, 1, tseq, carry_ref[...])
        carry_ref[...] = z[tm - 1:tm, :]
    else:
        zprev = _shifted(z, 1, tseq, [tail_ref[...]])
    zm = z + (zprev - z) * mu_ref[...]
    r = zm[:, :C_W]
    k = zm[:, C_W:2 * C_W]
    v = zm[:, 2 * C_W:3 * C_W]
    wd = zm[:, 3 * C_W:3 * C_W + LANE]
    ad = zm[:, 3 * C_W + LANE:3 * C_W + 2 * LANE]
    gd = zm[:, 3 * C_W + 2 * LANE:]
    w_log = -_softplus(-(w0_ref[...] + _dot(jnp.tanh(wd).astype(BF16), w2_ref[...]))) - 0.5
    log_decay = -jnp.exp(w_log)
    a = _sigmoid(a0_ref[...] + _dot(ad.astype(BF16), a2_ref[...]))
    g = _dot(_sigmoid(gd).astype(BF16), g2_ref[...])
    ones = _head_ones(C_W, DH_C)
    kk = k * kkw_ref[...]
    norm = jnp.sqrt(_dot3(kk * kk, ones))
    kk = kk / jnp.maximum(norm, 1e-12)
    k2 = k * (1.0 + (a - 1.0) * kaw_ref[...])
    r_ref[...] = r
    w_ref[...] = log_decay
    k_ref[...] = k2
    v_ref[...] = v
    kk_ref[...] = kk
    kka_ref[...] = kk * a
    g_ref[...] = g
    bonus_ref[...] = _dot3(r * k2 * rk_ref[...], ones) * v


def _rwkv_pre(z_c, tails, p, *, tseq):
    m = z_c.shape[0]
    tm = _tile(min(m, tseq) if tseq >= 256 else m, 256)
    row = lambda width: pl.BlockSpec((tm, width), lambda i: (i, 0))
    if tseq >= tm:
        nt = tseq // tm
        tail_spec = pl.BlockSpec((1, 1, C_PAD), lambda i: (i // nt, 0, 0))
    else:
        tail_spec = row(C_PAD)
    vec = _const_spec((1, C_W))
    return pl.pallas_call(
        functools.partial(_rwkv_pre_kernel, tseq=tseq), grid=(m // tm,),
        in_specs=[row(C_PAD), tail_spec, _const_spec((1, C_PAD)), vec, _const_spec((LANE, C_W)), vec,
                  _const_spec((LANE, C_W)), _const_spec((LANE, C_W)), vec, vec, vec],
        out_specs=[row(C_W)] * 8,
        out_shape=[jax.ShapeDtypeStruct((m, C_W), F32)] * 8,
        scratch_shapes=[pltpu.VMEM((1, C_PAD), F32)],
        compiler_params=_cp("arbitrary"), name="rwkv_pre",
    )(z_c, tails, p["mu"], p["w0"], p["w2"], p["a0"], p["a2"], p["g2"], p["kk"], p["ka"], p["rk"])


def _split2(x):
    hi = x.astype(BF16)
    return hi, (x - hi.astype(F32)).astype(BF16)


def _mm3(xh, xl, yh, yl):
    return _dot(xh, yh) + _dot(xh, yl) + _dot(xl, yh)


def _rwkv_chunk_kernel(r_ref, lw_ref, k_ref, v_ref, kk_ref, kka_ref, s0_ref, y_ref, so_ref, s_ref,
                       *, chunk, carry):
    n = r_ref.shape[0]
    if carry:
        @pl.when(pl.program_id(2) == 0)
        def _():
            s_ref[...] = s0_ref[0, 0]

    r, lw, k, v, kk, b = (x[...] for x in (r_ref, lw_ref, k_ref, v_ref, kk_ref, kka_ref))
    ri = lax.broadcasted_iota(jnp.int32, (n, n), 0)
    ci = lax.broadcasted_iota(jnp.int32, (n, n), 1)
    same = (ri // chunk) == (ci // chunk)
    incl = same & (ci <= ri)
    strict = same & (ci < ri)
    cl = _dot3_left(incl.astype(BF16), lw)
    tot = _dot3_left(same.astype(BF16), lw)
    e_inv = jnp.exp(-cl)
    e_end = jnp.exp(tot - cl)
    alpha = -kk * jnp.exp(cl - lw)
    rho = r * jnp.exp(cl)
    beta_end = b * e_end
    kappa_end = k * e_end
    lam_end = jnp.exp(tot)
    lane_head = lax.broadcasted_iota(jnp.int32, (n, LANE), 1) // DH_C
    vb = v.astype(BF16)
    rhs = jnp.concatenate([b * e_inv, k * e_inv], axis=0).astype(BF16)
    nt_dims = (((1,), (1,)), ((), ()))
    eye_n = (ri == ci).astype(F32)

    a_eff = jnp.zeros((n, LANE), F32)
    u0 = jnp.zeros((n, LANE), F32)
    p_eff = rho
    y0 = jnp.zeros((n, LANE), F32)
    for hh in range(LANE // DH_C):
        hmask = lane_head == hh
        a_h = jnp.where(hmask, alpha, 0.0)
        r_h = jnp.where(hmask, rho, 0.0)
        lhs = jnp.concatenate([a_h, r_h], axis=0).astype(BF16)
        g = lax.dot_general(lhs, rhs, nt_dims, preferred_element_type=F32)
        nmat = jnp.where(strict, g[:n, :n], 0.0)
        mmat = jnp.where(strict, g[:n, n:], 0.0).astype(BF16)
        pb = jnp.where(incl, g[n:, :n], 0.0).astype(BF16)
        pk = jnp.where(incl, g[n:, n:], 0.0).astype(BF16)
        ph, plo = _split2(nmat)
        tmat = eye_n + nmat
        for _ in range(int(math.log2(chunk)) - 1):
            ph, plo = _split2(_mm3(ph, plo, ph, plo))
            th, tl = _split2(tmat)
            tmat = tmat + _mm3(th, tl, ph, plo)
        th, tl = _split2(tmat)
        zin = jnp.concatenate([a_h, _dot(mmat, vb)], axis=1).astype(BF16)
        z = _dot(th, zin) + _dot(tl, zin)
        a_eff = a_eff + z[:, :LANE]
        u0 = jnp.where(hmask, z[:, LANE:], u0)
        w = _dot(pb, z.astype(BF16))
        p_eff = p_eff + w[:, :LANE]
        y0 = jnp.where(hmask, w[:, LANE:] + _dot(pk, vb), y0)

    ri2 = lax.broadcasted_iota(jnp.int32, (LANE, LANE), 0)
    ci2 = lax.broadcasted_iota(jnp.int32, (LANE, LANE), 1)
    same_head = (ri2 // DH_C) == (ci2 // DH_C)
    diag = ri2 == ci2
    col_chunk = lax.broadcasted_iota(jnp.int32, (LANE, n), 1) // chunk
    beta_t = beta_end.T
    kappa_t = kappa_end.T
    a_b = a_eff.astype(BF16)
    u0_b = u0.astype(BF16)
    if carry:
        s = s_ref[...]
    for c in range(n // chunk):
        rows = slice(c * chunk, (c + 1) * chunk)
        if not carry:
            s = s0_ref[c, 0]
        in_chunk = col_chunk == c
        bt = jnp.where(in_chunk, beta_t, 0.0).astype(BF16)
        kt = jnp.where(in_chunk, kappa_t, 0.0).astype(BF16)
        lam = jnp.broadcast_to(lam_end[c * chunk:c * chunk + 1, :], (LANE, LANE))
        dmat = jnp.where(same_head, _dot(bt, a_b), 0.0) + jnp.where(diag, lam, 0.0)
        emat = jnp.where(same_head, _dot(bt, u0_b) + _dot(kt, vb), 0.0)
        sh, sl = _split2(s)
        y_ref[rows, :] = _dot(p_eff[rows].astype(BF16), sh) + y0[rows]
        dh, dl = _split2(dmat)
        s = _mm3(dh, dl, sh, sl) + emat
        if not carry:
            so_ref[c, 0] = s
    if carry:
        s_ref[...] = s
        so_ref[0, 0] = s


def _rwkv_chunked(r, lw, k, v, kk, kka, s0_pairs, *, tseq):
    m = r.shape[0]
    n_seq = m // tseq
    n_pairs = C_W // LANE
    carry = tseq >= RW_SUPER
    if carry:
        n, chunk = RW_SUPER, RW_CHUNK
        assert tseq % n == 0
        nt = tseq // n
        grid = (n_seq, n_pairs, nt)
        row = pl.BlockSpec((n, LANE), lambda i, p, j: (i * nt + j, p))
        st = pl.BlockSpec((1, 1, LANE, LANE), lambda i, p, j: (i, p, 0, 0))
    else:
        chunk = tseq
        n = _tile(m, RW_SUPER)
        assert chunk & (chunk - 1) == 0 and chunk % SUBLANE == 0 and n % chunk == 0
        grid = (m // n, n_pairs, 1)
        row = pl.BlockSpec((n, LANE), lambda i, p, j: (i, p))
        st = pl.BlockSpec((n // chunk, 1, LANE, LANE), lambda i, p, j: (i, p, 0, 0))
    return pl.pallas_call(
        functools.partial(_rwkv_chunk_kernel, chunk=chunk, carry=carry), grid=grid,
        in_specs=[row] * 6 + [st],
        out_specs=[row, st],
        out_shape=[jax.ShapeDtypeStruct((m, C_W), F32), jax.ShapeDtypeStruct(s0_pairs.shape, F32)],
        scratch_shapes=[pltpu.VMEM((LANE, LANE), F32)],
        compiler_params=_cp("arbitrary", "arbitrary", "arbitrary"), name="rwkv_chunked",
    )(r, lw, k, v, kk, kka, s0_pairs)


def _state_to_pairs(s):
    n = s.shape[0]
    per = LANE // DH_C
    st = s.transpose(0, 1, 3, 2).reshape(n, H_C // per, per, DH_C, DH_C)
    eye = jnp.eye(per, dtype=s.dtype)
    bd = st[:, :, :, :, None, :] * eye[None, None, :, None, :, None]
    return bd.reshape(n, H_C // per, LANE, LANE)


def _pairs_to_state(sp):
    n = sp.shape[0]
    per = LANE // DH_C
    x = sp.reshape(n, H_C // per, per, DH_C, per, DH_C)
    d = jnp.stack([x[:, :, i, :, i, :] for i in range(per)], axis=2)
    return d.reshape(n, H_C, DH_C, DH_C).transpose(0, 1, 3, 2)


def _rwkv_post_kernel(y_ref, bonus_ref, g_ref, lnw_ref, lnb_ref, o_ref):
    y = y_ref[...]
    mean_mat = _head_ones(C_W, DH_C)
    mu = _dot3(y, mean_mat) * (1.0 / DH_C)
    yc = y - mu
    var = _dot3(yc * yc, mean_mat) * (1.0 / DH_C)
    yn = yc * lax.rsqrt(var + RWKV_GN_EPS) * lnw_ref[...] + lnb_ref[...]
    o_ref[...] = ((yn + bonus_ref[...]) * g_ref[...]).astype(o_ref.dtype)


def _rwkv_post(y, bonus, g, lnw, lnb):
    m = y.shape[0]
    tm = _tile(m, 512)
    row = pl.BlockSpec((tm, C_W), lambda i: (i, 0))
    vec = _const_spec((1, C_W))
    return pl.pallas_call(
        _rwkv_post_kernel, grid=(m // tm,),
        in_specs=[row, row, row, vec, vec], out_specs=row,
        out_shape=jax.ShapeDtypeStruct((m, C_W), BF16),
        compiler_params=_cp("parallel"), name="rwkv_post",
    )(y, bonus, g, lnw, lnb)


def _mla_pre_kernel(z_ref, qn_ref, kvn_ref, wq_ref, bdk_ref, cq_ref, sq_ref, ck_ref, sk_ref,
                    qa_ref, qp_ref, c_ref, kpe_ref):
    z = z_ref[...]
    qn = _rms(z[:, :Q_LORA], qn_ref[...]).astype(BF16)
    qf = _dot(qn, wq_ref[...])
    n_nope = H_D * NOPE_D
    n_pe = H_D * ROPE_D
    q_pe = qf[:, n_nope:n_nope + n_pe] * cq_ref[...] + qf[:, n_nope + n_pe:] * sq_ref[...]
    qa_ref[...] = _dot(qf[:, :n_nope].astype(BF16), bdk_ref[...]).astype(BF16)
    qp_ref[...] = q_pe.astype(BF16)
    c_ref[...] = _rms(z[:, Q_LORA:Q_LORA + KV_LORA], kvn_ref[...])
    o = Q_LORA + KV_LORA
    kpe_ref[...] = z[:, o:o + ROPE_D] * ck_ref[...] + z[:, o + ROPE_D:o + 2 * ROPE_D] * sk_ref[...]


def _mla_pre(z_d, p, cos_q, sin_q, cos_k, sin_k, *, tseq):
    m = z_d.shape[0]
    if tseq < 256:
        cos_q, sin_q, cos_k, sin_k = (jnp.tile(t, (m // tseq, 1)) for t in (cos_q, sin_q, cos_k, sin_k))
        tseq = m
    tm = _tile(min(m, tseq), 512)
    nt = tseq // tm
    row = lambda width: pl.BlockSpec((tm, width), lambda i: (i, 0))
    pos = lambda width: pl.BlockSpec((tm, width), lambda i: (i % nt, 0))
    n_pe = H_D * ROPE_D
    return pl.pallas_call(
        _mla_pre_kernel, grid=(m // tm,),
        in_specs=[row(D_PAD), _const_spec((1, Q_LORA)), _const_spec((1, KV_LORA)), _const_spec(p["wq"].shape),
                  _const_spec(p["bdk"].shape), pos(n_pe), pos(n_pe), pos(ROPE_D), pos(ROPE_D)],
        out_specs=[row(H_D * KV_LORA), row(n_pe), row(KV_LORA), row(ROPE_D)],
        out_shape=[jax.ShapeDtypeStruct((m, H_D * KV_LORA), BF16), jax.ShapeDtypeStruct((m, n_pe), BF16),
                   jax.ShapeDtypeStruct((m, KV_LORA), F32), jax.ShapeDtypeStruct((m, ROPE_D), F32)],
        compiler_params=_cp("parallel"), name="mla_pre",
    )(z_d, p["q_norm"], p["kv_norm"], p["wq"], p["bdk"], cos_q, sin_q, cos_k, sin_k)


def _ffn_kernel(h_ref, p_ref, tail_ref, *rest, tseq, final, n_tails):
    tail_refs = (tail_ref,) + rest[:n_tails - 1]
    (gf_ref, wup_ref, cw_ref, cb_ref, wdn_ref, gp_ref, wpg_ref, bpg_ref, wple_ref, gfin_ref,
     o_ref, buf_ref, carry_ref) = rest[n_tails - 1:]
    tm = h_ref.shape[0]
    taps = CONV_W - 1
    h = h_ref[...]
    xn = _rms(h, gf_ref[...]).astype(BF16)
    up = _dot(xn, wup_ref[...])
    a = up[:, :D_FF]
    b = up[:, D_FF:]
    if tseq >= tm:
        @pl.when((pl.program_id(0) * tm) % tseq == 0)
        def _():
            carry_ref[...] = tail_refs[0][0]
        tails = carry_ref[...]
    else:
        tails = [r[...] for r in tail_refs]
    c = cb_ref[...] + cw_ref[taps:taps + 1, :] * a
    for d in range(1, CONV_W):
        c = c + cw_ref[taps - d:taps - d + 1, :] * _shifted(a, d, tseq, tails)
    if tseq >= tm:
        carry_ref[...] = a[tm - taps:, :]
        buf_ref[0] = a[tm - taps:, :]
    else:
        buf_ref[...] = a.reshape(tm // tseq, tseq, D_FF)[:, tseq - taps:, :]
    f = _dot((_gelu(c) * b).astype(BF16), wdn_ref[...])
    h = h + f
    gate = _sigmoid(_dot(_rms(h, gp_ref[...]).astype(BF16), wpg_ref[...]) + bpg_ref[...])
    h = h + gate * _dot(p_ref[...].astype(BF16), wple_ref[...])
    if final:
        h = _rms(h, gfin_ref[...])
    o_ref[...] = h


def _ffn(h, p, tails, w, *, tseq, final):
    m = h.shape[0]
    n_seq = m // tseq
    taps = CONV_W - 1
    tm = _tile(min(m, tseq) if tseq >= 256 else m, 256)
    row = lambda width: pl.BlockSpec((tm, width), lambda i: (i, 0))
    single = lambda shape: pl.BlockSpec(shape, lambda *_: (0,) * len(shape), pipeline_mode=pl.Buffered(1))
    if tseq >= tm:
        nt = tseq // tm
        tail_specs = [pl.BlockSpec((1, taps, D_FF), lambda i: (i // nt, 0, 0))]
        buf_spec = pl.BlockSpec((1, taps, D_FF), lambda i: (i // nt, 0, 0))
    else:
        assert tseq >= taps
        tail_specs = [row(D_FF)] * taps
        buf_spec = pl.BlockSpec((tm // tseq, taps, D_FF), lambda i: (i, 0, 0))
    vec = _const_spec((1, D_MODEL))
    return pl.pallas_call(
        functools.partial(_ffn_kernel, tseq=tseq, final=final, n_tails=len(tail_specs)),
        grid=(m // tm,),
        in_specs=[row(D_MODEL), row(PLE_DIM)] + tail_specs + [
            vec, single((D_MODEL, 2 * D_FF)), _const_spec((CONV_W, D_FF)), _const_spec((1, D_FF)),
            single((D_FF, D_MODEL)), vec, single((D_MODEL, D_MODEL)), vec, single((PLE_DIM, D_MODEL)), vec],
        out_specs=[row(D_MODEL), buf_spec],
        out_shape=[jax.ShapeDtypeStruct((m, D_MODEL), F32), jax.ShapeDtypeStruct((n_seq, taps, D_FF), F32)],
        scratch_shapes=[pltpu.VMEM((taps, D_FF), F32)],
        compiler_params=_cp("arbitrary"), name="conv_ffn_ple",
    )(h, p, *tails, w["norm_ffn"], w["w_up"], w["conv_w"], w["conv_b"], w["w_down"], w["norm_ple"],
      w["w_pg"], w["b_pg"], w["w_ple"], w["norm_f"])


def _block_diag(blocks):
    n, r, c = blocks.shape
    eye = jnp.eye(n, dtype=blocks.dtype)
    return (eye[:, None, :, None] * blocks[:, :, None, :]).reshape(n * r, n * c)


def _pad_cols(x, width):
    return jnp.pad(x, [(0, 0)] * (x.ndim - 1) + [(0, width - x.shape[-1])])


def _rope_tables(pos, reps):
    half = ROPE_D // 2
    inv = ROPE_THETA ** (-jnp.arange(half, dtype=F32) / half)
    ang = pos.astype(F32)[:, None] * inv[None, :]
    cos = jnp.cos(ang)
    sin = jnp.sin(ang)
    cos = jnp.concatenate([cos, cos], axis=-1)
    sin = jnp.concatenate([-sin, sin], axis=-1)
    return jnp.tile(cos, (1, reps)), jnp.tile(sin, (1, reps))


def _swap_halves_cols(w, group):
    shp = w.shape
    w = w.reshape(shp[:-1] + (shp[-1] // group, 2, group // 2))
    return w[..., ::-1, :].reshape(shp)


def _even_weights(W, j):
    w = W["w_in_e"][j]
    o = 3 * A_W
    w_r = jnp.concatenate([w[:, :o], w[:, o + H_A:], _pad_cols(w[:, o:o + H_A], LANE)], axis=1).astype(BF16)
    ab, bbr, bbi = _s5_prep(
        W["s5_lam_re"][j].reshape(1, S5_W), W["s5_lam_im"][j].reshape(1, S5_W),
        jnp.repeat(W["s5_log_dt"][j], S5_N).reshape(1, S5_W),
        W["s5_b_re"][j].reshape(S5_W, S5_P).T, W["s5_b_im"][j].reshape(S5_W, S5_P).T)
    to_bd = lambda t: _block_diag(t.T.reshape(S5_GROUPS, S5_N, S5_P).transpose(0, 2, 1)).astype(BF16)
    to_cd = lambda c: _block_diag(c.transpose(0, 2, 1)).astype(BF16)
    return dict(
        w_in=w_r, b_f=_pad_cols(W["b_f"][j].reshape(1, H_A), LANE), ab=ab, bd_re=to_bd(bbr), bd_im=to_bd(bbi),
        cd_re=to_cd(W["s5_c_re"][j]), cd_im=to_cd(W["s5_c_im"][j]), d=W["s5_d"][j].reshape(1, B_W),
        w_glu=W["w_glu"][j].astype(BF16), b_glu=W["b_glu"][j].reshape(1, B_W),
        w_out=W["w_out_e"][j].astype(BF16))


def _pad_c(x):
    o = 3 * C_W
    return jnp.concatenate([x[..., :o], _pad_cols(x[..., o:o + W_LORA], LANE),
                            _pad_cols(x[..., o + W_LORA:o + W_LORA + A_LORA], LANE),
                            _pad_cols(x[..., o + W_LORA + A_LORA:], LANE)], axis=-1)


def _unpad_c(x):
    o = 3 * C_W
    return jnp.concatenate([x[..., :o], x[..., o:o + W_LORA], x[..., o + LANE:o + LANE + A_LORA],
                            x[..., o + 2 * LANE:o + 2 * LANE + G_LORA]], axis=-1)


def _pad_rows(x, rows):
    return jnp.pad(x, [(0, rows - x.shape[0]), (0, 0)])


def _odd_weights(W, j):
    w = W["w_in_o"][j]
    wd = w[:, C_IN:]
    o = Q_LORA + KV_LORA
    kpe_w = wd[:, o:o + ROPE_D]
    w_d = _pad_cols(jnp.concatenate([wd[:, :o], kpe_w, _swap_halves_cols(kpe_w, ROPE_D)], axis=1), D_PAD)
    w_r = jnp.concatenate([_pad_c(w[:, :C_IN]), w_d], axis=1).astype(BF16)
    wq = W["w_q_up"][j]
    wq_pe = wq[..., NOPE_D:].reshape(Q_LORA, H_D * ROPE_D)
    wq_r = jnp.concatenate([wq[..., :NOPE_D].reshape(Q_LORA, H_D * NOPE_D), wq_pe,
                            _swap_halves_cols(wq_pe, ROPE_D)], axis=1).astype(BF16)
    wkv = W["w_kv_up"][j]
    bdk = _block_diag(wkv[..., :NOPE_D].transpose(1, 2, 0)).astype(BF16)
    bdv = _block_diag(wkv[..., NOPE_D:].transpose(1, 0, 2)).astype(BF16)
    vec = lambda x: x.reshape(1, -1)
    rw = dict(mu=vec(_pad_c(W["rw_mu"][j])), w0=vec(W["rw_w0"][j]),
              w2=_pad_rows(W["rw_w2"][j], LANE).astype(BF16), a0=vec(W["rw_a0"][j]),
              a2=_pad_rows(W["rw_a2"][j], LANE).astype(BF16), g2=_pad_rows(W["rw_g2"][j], LANE).astype(BF16),
              kk=vec(W["rw_kk"][j]), ka=vec(W["rw_ka"][j]), rk=vec(W["rw_rk"][j]))
    mla = dict(q_norm=vec(W["mla_q_norm"][j]), kv_norm=vec(W["mla_kv_norm"][j]), wq=wq_r, bdk=bdk)
    return dict(w_in=w_r, rw=rw, mla=mla, lnw=vec(W["rw_lnw"][j]), lnb=vec(W["rw_lnb"][j]), bdv=bdv,
                w_out=W["w_out_o"][j].astype(BF16))


def _ffn_weights(W, i):
    vec = lambda x: x.reshape(1, -1)
    return dict(norm_ffn=vec(W["norm_ffn"][i]), w_up=W["w_ffn_up"][i].astype(BF16), conv_w=W["ffn_conv_w"][i],
                conv_b=vec(W["ffn_conv_b"][i]), w_down=W["w_ffn_down"][i].astype(BF16),
                norm_ple=vec(W["norm_ple"][i]), w_pg=W["w_pg"][i].astype(BF16), b_pg=vec(W["b_pg"][i]),
                w_ple=W["w_ple"][i].astype(BF16), norm_f=vec(W["norm_f"]))


def _fox_prompt(q, k, lf, v, n_seq, tseq):
    _, hi, mid, lo = _cumsum(lf.reshape(n_seq, tseq, LANE), tseq)
    heads = lambda x: x.reshape(n_seq, tseq, H_A, DH_A)
    col = lambda x: x[..., :H_A, None]
    one = jnp.ones((n_seq, tseq, H_A, 3), BF16)
    zero = jnp.zeros((n_seq, tseq, H_A, LANE - DH_A - 6), BF16)
    qa = jnp.concatenate([heads(q), col(hi), col(mid), col(lo), one, zero], axis=-1)
    ka = jnp.concatenate([heads(k.astype(BF16)), one, col(-hi), col(-mid), col(-lo), zero], axis=-1)
    to_hm = lambda x: x.transpose(0, 2, 1, 3).reshape(n_seq * H_A, tseq, x.shape[-1])
    o = _flash(to_hm(qa), to_hm(ka), to_hm(heads(v.astype(BF16))), rpt=1, tq=512, tk=512, scale=1.0)
    return o.reshape(n_seq, H_A, tseq, DH_A).transpose(0, 2, 1, 3).reshape(n_seq * tseq, A_W)


def _fox_decode(j, q, k, lf, v, n_seq, tseq, caches, page_table):
    cache_k, cache_v, cache_lf_t = caches
    cn = _cumsum(lf.reshape(1, n_seq * tseq, LANE), tseq)[0].reshape(n_seq, tseq, LANE)[..., :H_A]
    eye = jnp.eye(H_A, dtype=BF16)
    qh = q.reshape(n_seq, tseq, H_A, DH_A)
    q_bd = (qh[:, :, :, None, :] * eye[None, None, :, :, None]).reshape(n_seq, tseq * H_A, A_W)
    cn_rows = jnp.broadcast_to(cn.reshape(n_seq, tseq * H_A, 1), (n_seq, tseq * H_A, LANE))
    cn_keys = _pad_cols(cn.transpose(0, 2, 1), LANE)
    new_t = lambda x: _pad_cols(x.astype(BF16).reshape(n_seq, tseq, A_W).transpose(0, 2, 1), LANE)
    o = _fox_sample(j, page_table, q_bd, new_t(k), new_t(v), cn_rows, cn_keys, cache_k, cache_v, cache_lf_t)
    return o.reshape(n_seq * tseq, A_W)


def _even_layer(h, ew, j, n_seq, tseq, s5_re, s5_im, fox_ctx):
    q, k, v, u, lf = _even_in(h, ew["norm_mix"], ew["w_in"], ew["b_f"])
    if fox_ctx is None:
        o_a = _fox_prompt(q, k, lf, v, n_seq, tseq)
    else:
        o_a = _fox_decode(j, q, k, lf, v, n_seq, tseq, *fox_ctx)
    y_b, n_re, n_im = _s5(u, ew["ab"], ew["bd_re"], ew["bd_im"], ew["cd_re"], ew["cd_im"], ew["d"],
                          ew["w_glu"], ew["b_glu"], s5_re.reshape(n_seq, S5_W), s5_im.reshape(n_seq, S5_W),
                          n_seq=n_seq, tseq=tseq)
    h = _even_out(h, o_a, y_b, ew["w_out"])
    rows = (k.reshape(n_seq, tseq, H_A, DH_A), v.reshape(n_seq, tseq, H_A, DH_A),
            lf[:, :H_A].reshape(n_seq, tseq, H_A))
    st = (n_re.reshape(n_seq, S5_GROUPS, S5_N), n_im.reshape(n_seq, S5_GROUPS, S5_N))
    return h, rows, st


def _odd_layer(h, ow, j, n_seq, tseq, pos, rw_state, rw_shift, mla_ctx):
    m = h.shape[0]
    z_c, z_d = _odd_in(h, ow["norm_mix"], ow["w_in"])
    shift_pad = _pad_c(rw_shift.astype(F32))
    if tseq >= 256:
        tails = shift_pad.reshape(n_seq, 1, C_PAD)
    else:
        tails = jnp.repeat(shift_pad, tseq, axis=0)
    r, w, k, v, kk, kka, g, bonus = _rwkv_pre(z_c, tails, ow["rw"], tseq=tseq)
    y, s_pairs = _rwkv_chunked(r, w, k, v, kk, kka, _state_to_pairs(rw_state.astype(F32)), tseq=tseq)
    s_new = _pairs_to_state(s_pairs)
    o_c = _rwkv_post(y, bonus, g, ow["lnw"], ow["lnb"])
    sh = _unpad_c(z_c.reshape(n_seq, tseq, C_PAD)[:, -1])

    cos_q, sin_q = _rope_tables(pos, H_D)
    cos_k, sin_k = _rope_tables(pos, 1)
    q_abs, q_pe, c, kpe = _mla_pre(z_d, ow["mla"], cos_q, sin_q, cos_k, sin_k, tseq=tseq)
    if mla_ctx is None:
        rows = tseq * H_D
        qq = jnp.concatenate([q_abs.reshape(n_seq, rows, KV_LORA), q_pe.reshape(n_seq, rows, ROPE_D),
                              jnp.zeros((n_seq, rows, QK_PAD - KV_LORA - ROPE_D), BF16)], axis=-1)
        cb = c.astype(BF16).reshape(n_seq, tseq, KV_LORA)
        kq = jnp.concatenate([cb, kpe.astype(BF16).reshape(n_seq, tseq, ROPE_D),
                              jnp.zeros((n_seq, tseq, QK_PAD - KV_LORA - ROPE_D), BF16)], axis=-1)
        lat = _flash(qq, kq, cb, rpt=H_D, tq=128, tk=512, scale=(NOPE_D + ROPE_D) ** -0.5)
    else:
        cache_ckv, cache_kpe, page_table = mla_ctx
        rows = tseq * H_D
        c_new = jnp.pad(c.astype(BF16).reshape(n_seq, tseq, KV_LORA), ((0, 0), (0, LANE - tseq), (0, 0)))
        kpe_new_t = _pad_cols(kpe.astype(BF16).reshape(n_seq, tseq, ROPE_D).transpose(0, 2, 1), LANE)
        lat = _mla_sample(j, page_table, q_abs.reshape(n_seq, rows, KV_LORA), q_pe.reshape(n_seq, rows, ROPE_D),
                          c_new, kpe_new_t, cache_ckv, cache_kpe)
    h = _odd_out(h, o_c, lat.reshape(m, H_D * KV_LORA), ow["bdv"], ow["w_out"])
    return h, (s_new, sh), (c.reshape(n_seq, tseq, KV_LORA), kpe.reshape(n_seq, tseq, ROPE_D))


def _ffn_layer(h, p, buf, fw, n_seq, tseq, final):
    taps = CONV_W - 1
    if tseq >= 256:
        tails = [buf.astype(F32)]
    else:
        tails = [jnp.repeat(buf[:, i].astype(F32), tseq, axis=0) for i in range(taps)]
    return _ffn(h, p, tails, fw, tseq=tseq, final=final)


def _trunk(x, p, pos, s5_re, s5_im, rw_state, rw_shift, ffn_buf, weights, fox_ctx, mla_ctx):
    n_seq, tseq, _ = x.shape
    depth = p.shape[0]
    m = n_seq * tseq
    h = x.reshape(m, D_MODEL)
    fox_rows, s5_states, rw_states, mla_rows, ffn_bufs = [], [], [], [], []
    for i in range(depth):
        j = i // 2
        lw = weights["mix"][i]
        if i % 2 == 0:
            h, rows, st = _even_layer(h, lw, j, n_seq, tseq, s5_re[j], s5_im[j], fox_ctx)
            fox_rows.append(rows)
            s5_states.append(st)
        else:
            h, st, rows = _odd_layer(h, lw, j, n_seq, tseq, pos, rw_state[j], rw_shift[j], mla_ctx)
            rw_states.append(st)
            mla_rows.append(rows)
        h, nb = _ffn_layer(h, p[i].reshape(m, PLE_DIM), ffn_buf[i], weights["ffn"][i], n_seq, tseq,
                           final=(i == depth - 1))
        ffn_bufs.append(nb)
    stk = lambda xs, n: jnp.stack([r[n] for r in xs])
    return (h.reshape(n_seq, tseq, D_MODEL), stk(fox_rows, 0), stk(fox_rows, 1), stk(fox_rows, 2),
            stk(s5_states, 0), stk(s5_states, 1), stk(rw_states, 0), stk(rw_states, 1),
            stk(mla_rows, 0), stk(mla_rows, 1), jnp.stack(ffn_bufs))


def kernel(x_prompt, x_sample, cache_fox_k, cache_fox_v, cache_fox_lf, state_s5_re, state_s5_im, state_rwkv, state_shift, cache_mla_ckv, cache_mla_kpe, state_ffn_conv, page_table, p_prompt, p_sample, norm_mix, w_in_e, b_f, s5_lam_re, s5_lam_im, s5_log_dt, s5_b_re, s5_b_im, s5_c_re, s5_c_im, s5_d, w_glu, b_glu, w_out_e, w_in_o, rw_mu, rw_w0, rw_w2, rw_a0, rw_a2, rw_g2, rw_kk, rw_ka, rw_rk, rw_lnw, rw_lnb, mla_q_norm, w_q_up, mla_kv_norm, w_kv_up, w_out_o, norm_ffn, w_ffn_up, ffn_conv_w, ffn_conv_b, w_ffn_down, norm_ple, w_pg, b_pg, w_ple, norm_f):
    W = dict(w_in_e=w_in_e, b_f=b_f, s5_lam_re=s5_lam_re, s5_lam_im=s5_lam_im, s5_log_dt=s5_log_dt,
             s5_b_re=s5_b_re, s5_b_im=s5_b_im, s5_c_re=s5_c_re, s5_c_im=s5_c_im, s5_d=s5_d, w_glu=w_glu,
             b_glu=b_glu, w_out_e=w_out_e, w_in_o=w_in_o, rw_mu=rw_mu, rw_w0=rw_w0, rw_w2=rw_w2, rw_a0=rw_a0,
             rw_a2=rw_a2, rw_g2=rw_g2, rw_kk=rw_kk, rw_ka=rw_ka, rw_rk=rw_rk, rw_lnw=rw_lnw, rw_lnb=rw_lnb,
             mla_q_norm=mla_q_norm, w_q_up=w_q_up, mla_kv_norm=mla_kv_norm, w_kv_up=w_kv_up, w_out_o=w_out_o,
             norm_ffn=norm_ffn, w_ffn_up=w_ffn_up, ffn_conv_w=ffn_conv_w, ffn_conv_b=ffn_conv_b,
             w_ffn_down=w_ffn_down, norm_ple=norm_ple, w_pg=w_pg, b_pg=b_pg, w_ple=w_ple, norm_f=norm_f)
    depth = p_prompt.shape[0]
    ne = (depth + 1) // 2
    no = depth // 2
    mix = []
    for i in range(depth):
        lw = _even_weights(W, i // 2) if i % 2 == 0 else _odd_weights(W, i // 2)
        lw["norm_mix"] = norm_mix[i].reshape(1, D_MODEL)
        mix.append(lw)
    weights = dict(mix=mix, ffn=[_ffn_weights(W, i) for i in range(depth)])

    bp, sp, _ = x_prompt.shape
    zeros = lambda *shape: jnp.zeros(shape, F32)
    out_p = _trunk(x_prompt, p_prompt, jnp.arange(sp), zeros(ne, bp, S5_GROUPS, S5_N),
                   zeros(ne, bp, S5_GROUPS, S5_N), zeros(no, bp, H_C, DH_C, DH_C), zeros(no, bp, C_IN),
                   zeros(depth, bp, CONV_W - 1, D_FF), weights, None, None)

    n_pool, page = cache_fox_k.shape[1], cache_fox_k.shape[2]
    past = page_table.shape[1] * page
    tok_minor = lambda x: x.transpose(0, 1, 3, 4, 2).reshape(ne, n_pool, A_W, page)
    fox_ctx = ((tok_minor(cache_fox_k), tok_minor(cache_fox_v), cache_fox_lf.astype(F32).transpose(0, 1, 3, 2)),
               page_table)
    mla_ctx = (cache_mla_ckv, cache_mla_kpe.transpose(0, 1, 3, 2), page_table)
    out_s = _trunk(x_sample, p_sample, past + jnp.arange(x_sample.shape[1]), state_s5_re, state_s5_im,
                   state_rwkv, state_shift, state_ffn_conv, weights, fox_ctx, mla_ctx)

    (y_p, fk_p, fv_p, fl_p, sr_p, si_p, rw_p, sh_p, ck_p, kp_p, ff_p) = out_p
    (y_s, fk_s, fv_s, fl_s, sr_s, si_s, rw_s, sh_s, ck_s, kp_s, ff_s) = out_s
    return (y_p, y_s, fk_p, fv_p, fl_p, fk_s, fv_s, fl_s, sr_p, si_p, sr_s, si_s, rw_p, sh_p, rw_s, sh_s,
            ck_p, kp_p, ck_s, kp_s, ff_p, ff_s)
```

```python
import functools
import math

import jax
import jax.numpy as jnp
from jax import lax
from jax.experimental import pallas as pl
from jax.experimental.pallas import tpu as pltpu

F32 = jnp.float32
BF16 = jnp.bfloat16

D_MODEL = 1024
H_A = 8
DH_A = 64
A_W = H_A * DH_A
S5_GROUPS = 32
S5_P = 16
S5_N = 64
B_W = S5_GROUPS * S5_P
S5_W = S5_GROUPS * S5_N
H_C = 8
DH_C = 64
C_W = H_C * DH_C
W_LORA = 32
A_LORA = 32
G_LORA = 96
C_IN = 3 * C_W + W_LORA + A_LORA + G_LORA
RWKV_GN_EPS = 64e-5
H_D = 8
Q_LORA = 256
KV_LORA = 128
NOPE_D = 64
ROPE_D = 32
V_D = 64
ROPE_THETA = 10000.0
D_FF = 2816
CONV_W = 3
PLE_DIM = 256
NORM_EPS = 1e-6
NEG = -1e30

LANE = 128
SUBLANE = 8
VMEM_LIMIT = 56 * 1024 * 1024
C_PAD = 3 * C_W + 3 * LANE
D_PAD = 512
QK_PAD = 256
FOX_PAGES_PER_STEP = 16
MLA_PAGES_PER_STEP = 32
S5_SCAN_LANES = 512
RW_CHUNK = 32
RW_SUPER = 256
RW_PAIRS_PER_STEP = 1
S5_BLOCK_IN = LANE
S5_BLOCK_ST = LANE // S5_P * S5_N


def _cp(*sem):
    return pltpu.CompilerParams(dimension_semantics=sem, vmem_limit_bytes=VMEM_LIMIT)


def _tile(n, pref):
    if n <= pref:
        return n
    t = pref - pref % SUBLANE
    while t >= SUBLANE:
        if n % t == 0:
            return t
        t -= SUBLANE
    return n


def _const_spec(shape):
    nd = len(shape)
    return pl.BlockSpec(shape, lambda *_: (0,) * nd)


def _rms(x, g):
    return x * lax.rsqrt(jnp.mean(x * x, axis=-1, keepdims=True) + NORM_EPS) * g


def _split3(x):
    hi = x.astype(BF16)
    r1 = x - hi.astype(F32)
    mid = r1.astype(BF16)
    lo = (r1 - mid.astype(F32)).astype(BF16)
    return hi, mid, lo


def _dot(a, b):
    return jnp.dot(a, b, preferred_element_type=F32)


def _dot3(x, m):
    hi, mid, lo = _split3(x)
    return _dot(hi, m) + _dot(mid, m) + _dot(lo, m)


def _dot3_left(m, x):
    hi, mid, lo = _split3(x)
    return _dot(m, hi) + _dot(m, mid) + _dot(m, lo)


def _log_sigmoid(x):
    return jnp.minimum(x, 0.0) - jnp.log1p(jnp.exp(-jnp.abs(x)))


def _sigmoid(x):
    return 1.0 / (1.0 + jnp.exp(-x))


def _softplus(x):
    return jnp.maximum(x, 0.0) + jnp.log1p(jnp.exp(-jnp.abs(x)))


def _gelu(x):
    return jax.nn.gelu(x, approximate=True)


def _head_ones(width, head):
    r = lax.broadcasted_iota(jnp.int32, (width, width), 0) // head
    c = lax.broadcasted_iota(jnp.int32, (width, width), 1) // head
    return (r == c).astype(BF16)


def _even_in_kernel(h_ref, g_ref, w_ref, bf_ref, q_ref, k_ref, v_ref, u_ref, lf_ref):
    xn = _rms(h_ref[...], g_ref[...]).astype(BF16)
    z = _dot(xn, w_ref[...])
    q_ref[...] = (z[:, :A_W] * (DH_A ** -0.5)).astype(BF16)
    k_ref[...] = z[:, A_W:2 * A_W]
    v_ref[...] = z[:, 2 * A_W:3 * A_W]
    u_ref[...] = z[:, 3 * A_W:3 * A_W + B_W]
    lf_ref[...] = _log_sigmoid(z[:, 3 * A_W + B_W:] + bf_ref[...])


def _even_in(h, g, w, bf):
    m = h.shape[0]
    tm = _tile(m, 512)
    n = w.shape[1]
    row = lambda width: pl.BlockSpec((tm, width), lambda i: (i, 0))
    return pl.pallas_call(
        _even_in_kernel,
        grid=(m // tm,),
        in_specs=[row(D_MODEL), _const_spec((1, D_MODEL)), _const_spec((D_MODEL, n)), _const_spec((1, LANE))],
        out_specs=[row(A_W), row(A_W), row(A_W), row(B_W), row(LANE)],
        out_shape=[jax.ShapeDtypeStruct((m, A_W), BF16), jax.ShapeDtypeStruct((m, A_W), F32),
                   jax.ShapeDtypeStruct((m, A_W), F32), jax.ShapeDtypeStruct((m, B_W), F32),
                   jax.ShapeDtypeStruct((m, LANE), F32)],
        compiler_params=_cp("parallel"),
        name="even_in",
    )(h, g, w, bf)


def _cumsum_kernel(x_ref, c_ref, hi_ref, mid_ref, lo_ref, carry_ref, *, tseq):
    tc = x_ref.shape[1]
    ri = lax.broadcasted_iota(jnp.int32, (tc, tc), 0)
    ci = lax.broadcasted_iota(jnp.int32, (tc, tc), 1)
    if tseq >= tc:
        tri = (ri >= ci).astype(BF16)

        @pl.when((pl.program_id(1) * tc) % tseq == 0)
        def _():
            carry_ref[...] = jnp.zeros_like(carry_ref)
    else:
        tri = ((ri >= ci) & (ri // tseq == ci // tseq)).astype(BF16)
    c = _dot3_left(tri, x_ref[0])
    if tseq >= tc:
        c = c + carry_ref[...]
        carry_ref[...] = c[tc - 1:tc, :]
    c_ref[0] = c
    hi, mid, lo = _split3(c)
    hi_ref[0] = hi
    mid_ref[0] = mid
    lo_ref[0] = lo


def _cumsum(x, tseq):
    g, length, _ = x.shape
    tc = _tile(length, 512)
    assert tseq % tc == 0 or tc % tseq == 0
    blk = pl.BlockSpec((1, tc, LANE), lambda i, j: (i, j, 0))
    return pl.pallas_call(
        functools.partial(_cumsum_kernel, tseq=tseq),
        grid=(g, length // tc),
        in_specs=[blk],
        out_specs=[blk] * 4,
        out_shape=[jax.ShapeDtypeStruct(x.shape, F32)] + [jax.ShapeDtypeStruct(x.shape, BF16)] * 3,
        scratch_shapes=[pltpu.VMEM((1, LANE), F32)],
        compiler_params=_cp("arbitrary", "arbitrary"),
        name="fox_cumsum",
    )(x)


def _flash_kernel(q_ref, k_ref, v_ref, o_ref, *, rpt, tk, scale, dv):
    rows = q_ref.shape[1]
    tq = rows // rpt
    t0 = pl.program_id(1) * tq
    q = q_ref[0]
    if scale == 1.0:
        factor, ex = 1.0, jnp.exp
    else:
        factor, ex = scale * math.log2(math.e), jnp.exp2

    ones_col = v_ref.shape[2] > dv

    def scores(kc):
        k0 = pl.multiple_of(kc * tk, tk)
        return lax.dot_general(q, k_ref[0, pl.ds(k0, tk), :], _NT, preferred_element_type=F32)

    def softmax_pv(kc, s, carry, masked):
        m, l, acc = carry
        k0 = pl.multiple_of(kc * tk, tk)
        vb = v_ref[0, pl.ds(k0, tk), :]
        if masked:
            tok = t0 + lax.broadcasted_iota(jnp.int32, (rows, tk), 0) // rpt
            key = k0 + lax.broadcasted_iota(jnp.int32, (rows, tk), 1)
            s = jnp.where(key <= tok, s, NEG)
        if factor != 1.0:
            s = s * factor
        m_new = jnp.maximum(m, jnp.max(s, axis=1, keepdims=True))
        p = ex(s - m_new)
        alpha = ex(m - m_new)
        if not ones_col:
            l = alpha * l + jnp.sum(p, axis=1, keepdims=True)
        acc = alpha * acc + _dot(p.astype(BF16), vb)
        return m_new, l, acc

    assert tq <= tk
    n_full = t0 // tk

    def pair(i, carry):
        s_a = scores(2 * i)
        s_b = scores(2 * i + 1)
        carry = softmax_pv(2 * i, s_a, carry, False)
        return softmax_pv(2 * i + 1, s_b, carry, False)

    init = (jnp.full((rows, 1), NEG, F32), jnp.zeros((rows, 1), F32), jnp.zeros((rows, v_ref.shape[2]), F32))
    carry = lax.fori_loop(0, n_full // 2, pair, init)
    carry = lax.cond(n_full % 2 == 1,
                     lambda c: softmax_pv(n_full - 1, scores(n_full - 1), c, False), lambda c: c, carry)
    _, l, acc = softmax_pv(n_full, scores(n_full), carry, True)
    if ones_col:
        l = acc[:, dv:dv + 1]
    o_ref[0] = (acc[:, :dv] / l).astype(o_ref.dtype)


def _flash(q, k, v, *, rpt, tq, tk, scale):
    g, rows_total, dq = q.shape
    s = k.shape[1]
    dv = v.shape[2]
    tq = min(tq, s)
    tk = min(tk, s)
    assert s % tq == 0 and s % tk == 0 and (tk % tq == 0 or tq % tk == 0)
    rows = tq * rpt
    if dv % LANE:
        dva = dv + LANE - dv % LANE
        v_aug = jnp.concatenate([v, jnp.ones((g, s, 1), BF16), jnp.zeros((g, s, dva - dv - 1), BF16)], axis=-1)
    else:
        dva, v_aug = dv, v
    return pl.pallas_call(
        functools.partial(_flash_kernel, rpt=rpt, tk=tk, scale=scale, dv=dv),
        grid=(g, s // tq),
        in_specs=[pl.BlockSpec((1, rows, dq), lambda i, j: (i, j, 0)),
                  pl.BlockSpec((1, s, dq), lambda i, j: (i, 0, 0)),
                  pl.BlockSpec((1, s, dva), lambda i, j: (i, 0, 0))],
        out_specs=pl.BlockSpec((1, rows, dv), lambda i, j: (i, j, 0)),
        out_shape=jax.ShapeDtypeStruct((g, rows_total, dv), BF16),
        compiler_params=_cp("parallel", "arbitrary"),
        name="flash_prompt",
    )(q, k, v_aug)


_NT = (((1,), (1,)), ((), ()))


def _online_update(s, vals, m_ref, l_ref, acc_ref, *, vals_transposed=False):
    m_old = m_ref[...]
    m_new = m_old
    for blk in s:
        m_new = jnp.maximum(m_new, jnp.max(blk, axis=1, keepdims=True))
    alpha = jnp.exp(m_old - m_new)
    l = alpha * l_ref[...]
    acc = alpha * acc_ref[...]
    for blk, val in zip(s, vals):
        p = jnp.exp(blk - m_new)
        l = l + jnp.sum(p, axis=1, keepdims=True)
        if vals_transposed:
            acc = acc + lax.dot_general(p.astype(BF16), val, _NT, preferred_element_type=F32)
        else:
            acc = acc + _dot(p.astype(BF16), val)
    m_ref[...] = m_new
    l_ref[...] = l
    acc_ref[...] = acc


def _new_key_mask(rows, t_new):
    tok = lax.broadcasted_iota(jnp.int32, (rows, LANE), 0) // (rows // t_new)
    key = lax.broadcasted_iota(jnp.int32, (rows, LANE), 1)
    return key <= tok


def _fox_sample_kernel(pt_ref, q_ref, kn_ref, vn_ref, cnr_ref, cnk_ref, *rest, npg, t_new):
    k_refs = rest[:npg]
    v_refs = rest[npg:2 * npg]
    lf_refs = rest[2 * npg:3 * npg]
    o_ref, m_ref, l_ref, acc_ref, carry_ref = rest[3 * npg:]
    c = pl.program_id(1)
    rows = q_ref.shape[1]
    q = q_ref[0]
    cn_rows = cnr_ref[0]

    @pl.when(c == 0)
    def _():
        m_ref[...] = jnp.full_like(m_ref, NEG)
        l_ref[...] = jnp.zeros_like(l_ref)
        acc_ref[...] = jnp.zeros_like(acc_ref)
        carry_ref[...] = jnp.zeros_like(carry_ref)
        s = _dot(q, kn_ref[0])
        s = s + cn_rows - jnp.tile(cnk_ref[0], (t_new, 1))
        s = jnp.where(_new_key_mask(rows, t_new), s, NEG)
        _online_update([s], [vn_ref[0]], m_ref, l_ref, acc_ref, vals_transposed=True)

    ri = lax.broadcasted_iota(jnp.int32, (LANE, LANE), 0)
    ci = lax.broadcasted_iota(jnp.int32, (LANE, LANE), 1)
    after = (ri > ci).astype(BF16)
    ones = jnp.ones((LANE, LANE), BF16)
    lf_all = jnp.concatenate([r[...] for r in lf_refs], axis=0)
    suffix = _dot3(lf_all, after)
    total = _dot3(lf_all, ones)
    carry = carry_ref[...]
    scores = [None] * npg
    for i in reversed(range(npg)):
        s = _dot(q, k_refs[i][...].astype(BF16))
        bias = suffix[i * H_A:(i + 1) * H_A] + carry
        scores[i] = s + cn_rows + jnp.tile(bias, (t_new, 1))
        carry = carry + total[i * H_A:(i + 1) * H_A]
    carry_ref[...] = carry
    _online_update(scores, [r[...].astype(BF16) for r in v_refs], m_ref, l_ref, acc_ref, vals_transposed=True)

    @pl.when(c == pl.num_programs(1) - 1)
    def _():
        out = acc_ref[...] / l_ref[...]
        rh = lax.broadcasted_iota(jnp.int32, out.shape, 0) % H_A
        ch = lax.broadcasted_iota(jnp.int32, out.shape, 1) // DH_A
        out = jnp.where(rh == ch, out, 0.0).reshape(t_new, H_A, A_W)
        o_ref[0] = jnp.sum(out, axis=1).astype(o_ref.dtype)


def _fox_sample(layer, page_table, q_bd, k_new_t, v_new_t, cn_rows, cn_keys, cache_k_t, cache_v_t, cache_lf_t):
    b, n_pages = page_table.shape
    page = cache_k_t.shape[3]
    npg = min(FOX_PAGES_PER_STEP, n_pages)
    assert page == LANE and n_pages % npg == 0
    n_chunks = n_pages // npg
    rows = q_bd.shape[1]
    t_new = rows // H_A

    def seq_spec(shape):
        return pl.BlockSpec((1,) + shape, lambda i, c, pt: (i, 0, 0))

    def page_spec(shape, idx):
        def imap(i, c, pt):
            return (layer, pt[i, (n_chunks - 1 - c) * npg + idx], 0, 0)
        return pl.BlockSpec((None, None) + shape, imap)

    in_specs = [seq_spec((rows, A_W)), seq_spec((A_W, LANE)), seq_spec((A_W, LANE)),
                seq_spec((rows, LANE)), seq_spec((H_A, LANE))]
    in_specs += [page_spec((A_W, page), i) for i in range(npg)]
    in_specs += [page_spec((A_W, page), i) for i in range(npg)]
    in_specs += [page_spec((H_A, page), i) for i in range(npg)]
    grid_spec = pltpu.PrefetchScalarGridSpec(
        num_scalar_prefetch=1, grid=(b, n_chunks), in_specs=in_specs,
        out_specs=pl.BlockSpec((1, t_new, A_W), lambda i, c, pt: (i, 0, 0)),
        scratch_shapes=[pltpu.VMEM((rows, 1), F32), pltpu.VMEM((rows, 1), F32),
                        pltpu.VMEM((rows, A_W), F32), pltpu.VMEM((H_A, LANE), F32)])
    return pl.pallas_call(
        functools.partial(_fox_sample_kernel, npg=npg, t_new=t_new),
        grid_spec=grid_spec,
        out_shape=jax.ShapeDtypeStruct((b, t_new, A_W), BF16),
        compiler_params=_cp("parallel", "arbitrary"),
        name="fox_sample",
    )(page_table, q_bd, k_new_t, v_new_t, cn_rows, cn_keys,
      *([cache_k_t] * npg), *([cache_v_t] * npg), *([cache_lf_t] * npg))


def _mla_sample_kernel(pt_ref, qa_ref, qp_ref, cn_ref, kn_ref, *rest, npg, t_new, scale):
    c_refs = rest[:npg]
    p_refs = rest[npg:2 * npg]
    o_ref, m_ref, l_ref, acc_ref = rest[2 * npg:]
    c = pl.program_id(1)
    rows = qa_ref.shape[1]
    qa = qa_ref[0]
    qp = qp_ref[0]

    def score(cb, pb_t):
        return (lax.dot_general(qa, cb, _NT, preferred_element_type=F32) + _dot(qp, pb_t)) * scale

    @pl.when(c == 0)
    def _():
        m_ref[...] = jnp.full_like(m_ref, NEG)
        l_ref[...] = jnp.zeros_like(l_ref)
        acc_ref[...] = jnp.zeros_like(acc_ref)
        s = jnp.where(_new_key_mask(rows, t_new), score(cn_ref[0], kn_ref[0]), NEG)
        _online_update([s], [cn_ref[0]], m_ref, l_ref, acc_ref)

    cbs = [r[...].astype(BF16) for r in c_refs]
    scores = [score(cb, r[...].astype(BF16)) for cb, r in zip(cbs, p_refs)]
    _online_update(scores, cbs, m_ref, l_ref, acc_ref)

    @pl.when(c == pl.num_programs(1) - 1)
    def _():
        o_ref[0] = (acc_ref[...] / l_ref[...]).astype(o_ref.dtype)


def _mla_sample(layer, page_table, q_abs, q_pe, c_new, kpe_new_t, cache_ckv, cache_kpe_t):
    b, n_pages = page_table.shape
    page = cache_ckv.shape[2]
    npg = min(MLA_PAGES_PER_STEP, n_pages)
    assert page == LANE and n_pages % npg == 0
    n_chunks = n_pages // npg
    rows = q_abs.shape[1]
    t_new = rows // H_D

    def seq_spec(shape):
        return pl.BlockSpec((1,) + shape, lambda i, c, pt: (i, 0, 0))

    def page_spec(shape, idx):
        def imap(i, c, pt):
            return (layer, pt[i, c * npg + idx], 0, 0)
        return pl.BlockSpec((None, None) + shape, imap)

    in_specs = [seq_spec((rows, KV_LORA)), seq_spec((rows, ROPE_D)),
                seq_spec((LANE, KV_LORA)), seq_spec((ROPE_D, LANE))]
    in_specs += [page_spec((page, KV_LORA), i) for i in range(npg)]
    in_specs += [page_spec((ROPE_D, page), i) for i in range(npg)]
    grid_spec = pltpu.PrefetchScalarGridSpec(
        num_scalar_prefetch=1, grid=(b, n_chunks), in_specs=in_specs,
        out_specs=pl.BlockSpec((1, rows, KV_LORA), lambda i, c, pt: (i, 0, 0)),
        scratch_shapes=[pltpu.VMEM((rows, 1), F32), pltpu.VMEM((rows, 1), F32),
                        pltpu.VMEM((rows, KV_LORA), F32)])
    return pl.pallas_call(
        functools.partial(_mla_sample_kernel, npg=npg, t_new=t_new, scale=(NOPE_D + ROPE_D) ** -0.5),
        grid_spec=grid_spec,
        out_shape=jax.ShapeDtypeStruct((b, rows, KV_LORA), BF16),
        compiler_params=_cp("parallel", "arbitrary"),
        name="mla_sample",
    )(page_table, q_abs, q_pe, c_new, kpe_new_t, *([cache_ckv] * npg), *([cache_kpe_t] * npg))


def _s5_prep_kernel(lr_ref, li_ref, ldt_ref, br_ref, bi_ref, ab_ref, bbr_ref, bbi_ref):
    lr = lr_ref[...]
    li = li_ref[...]
    dt = jnp.exp(ldt_ref[...])
    mag = jnp.exp(lr * dt)
    ang = li * dt
    ab_re = mag * jnp.cos(ang)
    ab_im = mag * jnp.sin(ang)
    den = lr * lr + li * li
    f_re = ((ab_re - 1.0) * lr + ab_im * li) / den
    f_im = (ab_im * lr - (ab_re - 1.0) * li) / den
    ab_ref[0:1, :] = ab_re
    ab_ref[1:2, :] = ab_im
    br = br_ref[...]
    bi = bi_ref[...]
    bbr_ref[...] = f_re * br - f_im * bi
    bbi_ref[...] = f_re * bi + f_im * br


def _s5_prep(lam_re, lam_im, log_dt, b_re_t, b_im_t):
    return pl.pallas_call(
        _s5_prep_kernel,
        out_shape=[jax.ShapeDtypeStruct((2, S5_W), F32), jax.ShapeDtypeStruct((S5_P, S5_W), F32),
                   jax.ShapeDtypeStruct((S5_P, S5_W), F32)],
        name="s5_prep",
    )(lam_re, lam_im, log_dt, b_re_t, b_im_t)


def _cmul(ar, ai, br, bi):
    return ar * br - ai * bi, ar * bi + ai * br


def _s5_kernel(u_ref, ab_ref, bdr_ref, bdi_ref, cdr_ref, cdi_ref, d_ref, wg_ref, bg_ref, h0r_ref, h0i_ref,
               y_ref, sr_ref, si_ref, xr_ref, xi_ref, car_ref, cai_ref, *, grouped):
    tc = u_ref.shape[0]
    u = u_ref[...]
    ub = u.astype(BF16)
    n_blk = B_W // S5_BLOCK_IN
    for j in range(n_blk):
        ui = slice(j * S5_BLOCK_IN, (j + 1) * S5_BLOCK_IN)
        xs = slice(j * S5_BLOCK_ST, (j + 1) * S5_BLOCK_ST)
        xr_ref[:, xs] = _dot(ub[:, ui], bdr_ref[ui, xs])
        xi_ref[:, xs] = _dot(ub[:, ui], bdi_ref[ui, xs])

    if not grouped:
        @pl.when(pl.program_id(1) == 0)
        def _():
            car_ref[...] = h0r_ref[0]
            cai_ref[...] = h0i_ref[0]

    cw = S5_SCAN_LANES
    tile = (SUBLANE, cw)
    row = lax.broadcasted_iota(jnp.int32, tile, 0)
    n_groups = tc // SUBLANE
    for j in range(S5_W // cw):
        lanes = slice(j * cw, (j + 1) * cw)
        a1r = jnp.broadcast_to(ab_ref[0:1, lanes], tile)
        a1i = jnp.broadcast_to(ab_ref[1:2, lanes], tile)
        a2r, a2i = _cmul(a1r, a1i, a1r, a1i)
        a4r, a4i = _cmul(a2r, a2i, a2r, a2i)
        apr, api = a1r, a1i
        for d, (pr, pi) in ((1, (a1r, a1i)), (2, (a2r, a2i)), (4, (a4r, a4i))):
            sr = pltpu.roll(apr, d, axis=0)
            si = pltpu.roll(api, d, axis=0)
            mr, mi = _cmul(pr, pi, sr, si)
            apr = jnp.where(row >= d, mr, apr)
            api = jnp.where(row >= d, mi, api)

        def body(g, carry):
            cr, ci = carry
            r0 = pl.multiple_of(g * SUBLANE, SUBLANE)
            xr = xr_ref[pl.ds(r0, SUBLANE), lanes]
            xi = xi_ref[pl.ds(r0, SUBLANE), lanes]
            for d, (pr, pi) in ((1, (a1r, a1i)), (2, (a2r, a2i)), (4, (a4r, a4i))):
                sr = jnp.where(row >= d, pltpu.roll(xr, d, axis=0), 0.0)
                si = jnp.where(row >= d, pltpu.roll(xi, d, axis=0), 0.0)
                mr, mi = _cmul(pr, pi, sr, si)
                xr = xr + mr
                xi = xi + mi
            if grouped:
                cr = jnp.broadcast_to(h0r_ref[pl.ds(g, 1), lanes], tile)
                ci = jnp.broadcast_to(h0i_ref[pl.ds(g, 1), lanes], tile)
            mr, mi = _cmul(apr, api, cr, ci)
            xr = xr + mr
            xi = xi + mi
            xr_ref[pl.ds(r0, SUBLANE), lanes] = xr
            xi_ref[pl.ds(r0, SUBLANE), lanes] = xi
            if grouped:
                sr_ref[pl.ds(g, 1), lanes] = xr[SUBLANE - 1:SUBLANE]
                si_ref[pl.ds(g, 1), lanes] = xi[SUBLANE - 1:SUBLANE]
                return cr, ci
            last_r = jnp.broadcast_to(xr[SUBLANE - 1:SUBLANE], tile)
            last_i = jnp.broadcast_to(xi[SUBLANE - 1:SUBLANE], tile)
            return last_r, last_i

        if grouped:
            init = (jnp.zeros(tile, F32), jnp.zeros(tile, F32))
        else:
            init = (jnp.broadcast_to(car_ref[0:1, lanes], tile), jnp.broadcast_to(cai_ref[0:1, lanes], tile))
        cr, ci = lax.fori_loop(0, n_groups, body, init)
        if not grouped:
            car_ref[0:1, lanes] = cr[0:1]
            cai_ref[0:1, lanes] = ci[0:1]
            sr_ref[0, 0:1, lanes] = cr[0:1]
            si_ref[0, 0:1, lanes] = ci[0:1]

    ys = []
    for j in range(n_blk):
        ui = slice(j * S5_BLOCK_IN, (j + 1) * S5_BLOCK_IN)
        xs = slice(j * S5_BLOCK_ST, (j + 1) * S5_BLOCK_ST)
        ys.append(_dot(xr_ref[:, xs].astype(BF16), cdr_ref[xs, ui])
                  - _dot(xi_ref[:, xs].astype(BF16), cdi_ref[xs, ui]))
    y = jnp.concatenate(ys, axis=1) + d_ref[...] * u
    y = _gelu(y)
    y = y * _sigmoid(_dot(y.astype(BF16), wg_ref[...]) + bg_ref[...])
    y_ref[...] = y.astype(y_ref.dtype)


def _s5(u, ab, bd_re, bd_im, cd_re, cd_im, d, w_glu, b_glu, h0_re, h0_im, *, n_seq, tseq):
    m = u.shape[0]
    grouped = tseq == SUBLANE
    consts = [_const_spec((2, S5_W)), _const_spec((B_W, S5_W)), _const_spec((B_W, S5_W)),
              _const_spec((S5_W, B_W)), _const_spec((S5_W, B_W)), _const_spec((1, B_W)),
              _const_spec((B_W, B_W)), _const_spec((1, B_W))]
    if grouped:
        tc = _tile(m, 256)
        ng = tc // SUBLANE
        grid = (m // tc, 1)
        u_spec = pl.BlockSpec((tc, B_W), lambda i, j: (i, 0))
        st_spec = pl.BlockSpec((ng, S5_W), lambda i, j: (i, 0))
        h0_spec = st_spec
        st_shape = jax.ShapeDtypeStruct((n_seq, S5_W), F32)
    else:
        tc = _tile(tseq, 256)
        assert tc % SUBLANE == 0
        nt = tseq // tc
        grid = (n_seq, nt)
        u_spec = pl.BlockSpec((tc, B_W), lambda i, j: (i * nt + j, 0))
        st_spec = pl.BlockSpec((1, 1, S5_W), lambda i, j: (i, 0, 0))
        h0_spec = st_spec
        st_shape = jax.ShapeDtypeStruct((n_seq, 1, S5_W), F32)
        h0_re = h0_re.reshape(n_seq, 1, S5_W)
        h0_im = h0_im.reshape(n_seq, 1, S5_W)
    y, s_re, s_im = pl.pallas_call(
        functools.partial(_s5_kernel, grouped=grouped),
        grid=grid,
        in_specs=[u_spec] + consts + [h0_spec, h0_spec],
        out_specs=[u_spec, st_spec, st_spec],
        out_shape=[jax.ShapeDtypeStruct((m, B_W), BF16), st_shape, st_shape],
        scratch_shapes=[pltpu.VMEM((tc, S5_W), F32), pltpu.VMEM((tc, S5_W), F32),
                        pltpu.VMEM((1, S5_W), F32), pltpu.VMEM((1, S5_W), F32)],
        compiler_params=_cp("arbitrary", "arbitrary"),
        name="s5_mix",
    )(u, ab, bd_re, bd_im, cd_re, cd_im, d, w_glu, b_glu, h0_re, h0_im)
    return y, s_re.reshape(n_seq, S5_W), s_im.reshape(n_seq, S5_W)


def _even_out_kernel(h_ref, a_ref, b_ref, w_ref, o_ref):
    o_ref[...] = h_ref[...] + _dot(a_ref[...], w_ref[0:A_W, :]) + _dot(b_ref[...], w_ref[A_W:, :])


def _odd_out_kernel(h_ref, a_ref, lat_ref, bdv_ref, w_ref, o_ref):
    o_d = _dot(lat_ref[...], bdv_ref[...]).astype(BF16)
    o_ref[...] = h_ref[...] + _dot(a_ref[...], w_ref[0:C_W, :]) + _dot(o_d, w_ref[C_W:, :])


def _even_out(h, a, b, w):
    m = h.shape[0]
    tm = _tile(m, 512)
    row = lambda width: pl.BlockSpec((tm, width), lambda i: (i, 0))
    return pl.pallas_call(
        _even_out_kernel, grid=(m // tm,),
        in_specs=[row(D_MODEL), row(A_W), row(B_W), _const_spec(w.shape)],
        out_specs=row(D_MODEL), out_shape=jax.ShapeDtypeStruct((m, D_MODEL), F32),
        compiler_params=_cp("parallel"), name="even_out",
    )(h, a, b, w)


def _odd_out(h, a, lat, bdv, w):
    m = h.shape[0]
    tm = _tile(m, 512)
    row = lambda width: pl.BlockSpec((tm, width), lambda i: (i, 0))
    return pl.pallas_call(
        _odd_out_kernel, grid=(m // tm,),
        in_specs=[row(D_MODEL), row(C_W), row(H_D * KV_LORA), _const_spec(bdv.shape), _const_spec(w.shape)],
        out_specs=row(D_MODEL), out_shape=jax.ShapeDtypeStruct((m, D_MODEL), F32),
        compiler_params=_cp("parallel"), name="odd_out",
    )(h, a, lat, bdv, w)


def _shifted(x, d, tseq, tails):
    tm = x.shape[0]
    rolled = pltpu.roll(x, d, axis=0)
    row = lax.broadcasted_iota(jnp.int32, (tm, 1), 0)
    if tseq >= tm:
        taps = tails.shape[0]
        out = rolled
        for i in range(d):
            out = jnp.where(row == i, tails[taps - d + i:taps - d + i + 1, :], out)
        return out
    taps = len(tails)
    pos = row % tseq
    out = rolled
    for i in range(d):
        out = jnp.where(pos == i, tails[taps - d + i], out)
    return out


def _odd_in_kernel(h_ref, g_ref, w_ref, zc_ref, zd_ref):
    xn = _rms(h_ref[...], g_ref[...]).astype(BF16)
    z = _dot(xn, w_ref[...])
    zc_ref[...] = z[:, :C_PAD]
    zd_ref[...] = z[:, C_PAD:]


def _odd_in(h, g, w):
    m = h.shape[0]
    tm = _tile(m, 512)
    row = lambda width: pl.BlockSpec((tm, width), lambda i: (i, 0))
    return pl.pallas_call(
        _odd_in_kernel, grid=(m // tm,),
        in_specs=[row(D_MODEL), _const_spec((1, D_MODEL)), _const_spec(w.shape)],
        out_specs=[row(C_PAD), row(D_PAD)],
        out_shape=[jax.ShapeDtypeStruct((m, C_PAD), F32), jax.ShapeDtypeStruct((m, D_PAD), F32)],
        compiler_params=_cp("parallel"), name="odd_in",
    )(h, g, w)


def _rwkv_pre_kernel(z_ref, tail_ref, mu_ref, w0_ref, w2_ref, a0_ref, a2_ref, g2_ref, kkw_ref, kaw_ref,
                     rk_ref, r_ref, w_ref, k_ref, v_ref, kk_ref, kka_ref, g_ref, bonus_ref, carry_ref,
                     *, tseq):
    tm = z_ref.shape[0]
    z = z_ref[...]
    if tseq >= tm:
        @pl.when((pl.program_id(0) * tm) % tseq == 0)
        def _():
            carry_ref[...] = tail_ref[0]
        zprev = _shifted(z, 1, tseq, carry_ref[...])
        carry_ref[...] = z[tm - 1:tm, :]
    else:
        zprev = _shifted(z, 1, tseq, [tail_ref[...]])
    zm = z + (zprev - z) * mu_ref[...]
    r = zm[:, :C_W]
    k = zm[:, C_W:2 * C_W]
    v = zm[:, 2 * C_W:3 * C_W]
    wd = zm[:, 3 * C_W:3 * C_W + LANE]
    ad = zm[:, 3 * C_W + LANE:3 * C_W + 2 * LANE]
    gd = zm[:, 3 * C_W + 2 * LANE:]
    w_log = -_softplus(-(w0_ref[...] + _dot(jnp.tanh(wd).astype(BF16), w2_ref[...]))) - 0.5
    log_decay = -jnp.exp(w_log)
    a = _sigmoid(a0_ref[...] + _dot(ad.astype(BF16), a2_ref[...]))
    g = _dot(_sigmoid(gd).astype(BF16), g2_ref[...])
    ones = _head_ones(C_W, DH_C)
    kk = k * kkw_ref[...]
    norm = jnp.sqrt(_dot3(kk * kk, ones))
    kk = kk / jnp.maximum(norm, 1e-12)
    k2 = k * (1.0 + (a - 1.0) * kaw_ref[...])
    r_ref[...] = r
    w_ref[...] = log_decay
    k_ref[...] = k2
    v_ref[...] = v
    kk_ref[...] = kk
    kka_ref[...] = kk * a
    g_ref[...] = g
    bonus_ref[...] = _dot3(r * k2 * rk_ref[...], ones) * v


def _rwkv_pre(z_c, tails, p, *, tseq):
    m = z_c.shape[0]
    tm = _tile(min(m, tseq) if tseq >= 256 else m, 256)
    row = lambda width: pl.BlockSpec((tm, width), lambda i: (i, 0))
    if tseq >= tm:
        nt = tseq // tm
        tail_spec = pl.BlockSpec((1, 1, C_PAD), lambda i: (i // nt, 0, 0))
    else:
        tail_spec = row(C_PAD)
    vec = _const_spec((1, C_W))
    return pl.pallas_call(
        functools.partial(_rwkv_pre_kernel, tseq=tseq), grid=(m // tm,),
        in_specs=[row(C_PAD), tail_spec, _const_spec((1, C_PAD)), vec, _const_spec((LANE, C_W)), vec,
                  _const_spec((LANE, C_W)), _const_spec((LANE, C_W)), vec, vec, vec],
        out_specs=[row(C_W)] * 8,
        out_shape=[jax.ShapeDtypeStruct((m, C_W), F32)] * 8,
        scratch_shapes=[pltpu.VMEM((1, C_PAD), F32)],
        compiler_params=_cp("arbitrary"), name="rwkv_pre",
    )(z_c, tails, p["mu"], p["w0"], p["w2"], p["a0"], p["a2"], p["g2"], p["kk"], p["ka"], p["rk"])


def _split2(x):
    hi = x.astype(BF16)
    return hi, (x - hi.astype(F32)).astype(BF16)


def _mm3(xh, xl, yh, yl):
    return _dot(xh, yh) + _dot(xh, yl) + _dot(xl, yh)


def _rwkv_chunk_kernel(r_ref, lw_ref, k_ref, v_ref, kk_ref, kka_ref, s0_ref, y_ref, so_ref, s_ref,
                       *, chunk, carry):
    n = r_ref.shape[0]
    n_chunks = n // chunk
    if carry:
        @pl.when(pl.program_id(2) == 0)
        def _():
            s_ref[...] = s0_ref[0]

    ri = lax.broadcasted_iota(jnp.int32, (n, n), 0)
    ci = lax.broadcasted_iota(jnp.int32, (n, n), 1)
    same = (ri // chunk) == (ci // chunk)
    incl = same & (ci <= ri)
    strict = same & (ci < ri)
    incl_b = incl.astype(BF16)
    same_b = same.astype(BF16)
    eye_n = (ri == ci).astype(F32)
    lane_head = lax.broadcasted_iota(jnp.int32, (n, LANE), 1) // DH_C
    ri2 = lax.broadcasted_iota(jnp.int32, (LANE, LANE), 0)
    ci2 = lax.broadcasted_iota(jnp.int32, (LANE, LANE), 1)
    same_head = (ri2 // DH_C) == (ci2 // DH_C)
    diag = ri2 == ci2
    col_chunk = lax.broadcasted_iota(jnp.int32, (LANE, n), 1) // chunk
    stack = lambda x: jnp.concatenate([jnp.where(col_chunk == c, x, 0.0) for c in range(n_chunks)],
                                      axis=0).astype(BF16)

    for pp in range(r_ref.shape[1] // LANE):
        lanes = slice(pp * LANE, (pp + 1) * LANE)
        r, lw, k, v, kk, b = (x[:, lanes] for x in (r_ref, lw_ref, k_ref, v_ref, kk_ref, kka_ref))
        cl = _dot3_left(incl_b, lw)
        tot = _dot3_left(same_b, lw)
        e_inv = jnp.exp(-cl)
        e_end = jnp.exp(tot - cl)
        alpha = -kk * jnp.exp(cl - lw)
        rho = r * jnp.exp(cl)
        beta_end = b * e_end
        kappa_end = k * e_end
        lam_end = jnp.exp(tot)
        vb = v.astype(BF16)
        rhs = jnp.concatenate([b * e_inv, k * e_inv], axis=0).astype(BF16)

        a_eff = jnp.zeros((n, LANE), F32)
        u0 = jnp.zeros((n, LANE), F32)
        p_eff = rho
        y0 = jnp.zeros((n, LANE), F32)
        for hh in range(LANE // DH_C):
            hmask = lane_head == hh
            a_h = jnp.where(hmask, alpha, 0.0)
            r_h = jnp.where(hmask, rho, 0.0)
            lhs = jnp.concatenate([a_h, r_h], axis=0).astype(BF16)
            g = lax.dot_general(lhs, rhs, _NT, preferred_element_type=F32)
            nmat = jnp.where(strict, g[:n, :n], 0.0)
            mmat = jnp.where(strict, g[:n, n:], 0.0).astype(BF16)
            pb = jnp.where(incl, g[n:, :n], 0.0).astype(BF16)
            pk = jnp.where(incl, g[n:, n:], 0.0).astype(BF16)
            ph = nmat.astype(BF16)
            tmat = eye_n + nmat
            for _ in range(int(math.log2(chunk)) - 1):
                ph = _dot(ph, ph).astype(BF16)
                tmat = tmat + _dot(tmat.astype(BF16), ph)
            th, tl = _split2(tmat)
            zin = jnp.concatenate([a_h, _dot(mmat, vb)], axis=1).astype(BF16)
            z = _dot(th, zin) + _dot(tl, zin)
            a_eff = a_eff + z[:, :LANE]
            u0 = jnp.where(hmask, z[:, LANE:], u0)
            w = _dot(pb, z.astype(BF16))
            p_eff = p_eff + w[:, :LANE]
            y0 = jnp.where(hmask, w[:, LANE:] + _dot(pk, vb), y0)

        a_b = a_eff.astype(BF16)
        u0_b = u0.astype(BF16)
        bt_all = stack(beta_end.T)
        g_all = _dot(bt_all, a_b)
        e_all = _dot(bt_all, u0_b) + _dot(stack(kappa_end.T), vb)
        if carry:
            s = s_ref[pp]
        for c in range(n_chunks):
            rows = slice(c * chunk, (c + 1) * chunk)
            blk = slice(c * LANE, (c + 1) * LANE)
            if not carry:
                s = s0_ref[c, pp]
            lam = jnp.broadcast_to(lam_end[c * chunk:c * chunk + 1, :], (LANE, LANE))
            dmat = jnp.where(same_head, g_all[blk], 0.0) + jnp.where(diag, lam, 0.0)
            emat = jnp.where(same_head, e_all[blk], 0.0)
            sh, sl = _split2(s)
            y_ref[rows, lanes] = _dot(p_eff[rows].astype(BF16), sh) + y0[rows]
            dh, dl = _split2(dmat)
            s = _mm3(dh, dl, sh, sl) + emat
            if not carry:
                so_ref[c, pp] = s
        if carry:
            s_ref[pp] = s
            so_ref[0, pp] = s


def _rwkv_chunked(r, lw, k, v, kk, kka, s0_pairs, *, tseq):
    m = r.shape[0]
    n_seq = m // tseq
    pps = RW_PAIRS_PER_STEP
    n_groups = C_W // (LANE * pps)
    carry = tseq >= RW_SUPER
    if carry:
        n, chunk = RW_SUPER, RW_CHUNK
        assert tseq % n == 0
        nt = tseq // n
        grid = (n_seq, n_groups, nt)
        row = pl.BlockSpec((n, pps * LANE), lambda i, p, j: (i * nt + j, p))
        st = pl.BlockSpec((1, pps, LANE, LANE), lambda i, p, j: (i, p, 0, 0))
    else:
        chunk = tseq
        n = _tile(m, RW_SUPER)
        assert chunk & (chunk - 1) == 0 and chunk % SUBLANE == 0 and n % chunk == 0
        grid = (m // n, n_groups, 1)
        row = pl.BlockSpec((n, pps * LANE), lambda i, p, j: (i, p))
        st = pl.BlockSpec((n // chunk, pps, LANE, LANE), lambda i, p, j: (i, p, 0, 0))
    return pl.pallas_call(
        functools.partial(_rwkv_chunk_kernel, chunk=chunk, carry=carry), grid=grid,
        in_specs=[row] * 6 + [st],
        out_specs=[row, st],
        out_shape=[jax.ShapeDtypeStruct((m, C_W), F32), jax.ShapeDtypeStruct(s0_pairs.shape, F32)],
        scratch_shapes=[pltpu.VMEM((pps, LANE, LANE), F32)],
        compiler_params=_cp("arbitrary", "arbitrary", "arbitrary"), name="rwkv_chunked",
    )(r, lw, k, v, kk, kka, s0_pairs)


def _state_to_pairs(s):
    n = s.shape[0]
    per = LANE // DH_C
    st = s.transpose(0, 1, 3, 2).reshape(n, H_C // per, per, DH_C, DH_C)
    eye = jnp.eye(per, dtype=s.dtype)
    bd = st[:, :, :, :, None, :] * eye[None, None, :, None, :, None]
    return bd.reshape(n, H_C // per, LANE, LANE)


def _pairs_to_state(sp):
    n = sp.shape[0]
    per = LANE // DH_C
    x = sp.reshape(n, H_C // per, per, DH_C, per, DH_C)
    d = jnp.stack([x[:, :, i, :, i, :] for i in range(per)], axis=2)
    return d.reshape(n, H_C, DH_C, DH_C).transpose(0, 1, 3, 2)


def _rwkv_post_kernel(y_ref, bonus_ref, g_ref, lnw_ref, lnb_ref, o_ref):
    y = y_ref[...]
    mean_mat = _head_ones(C_W, DH_C)
    mu = _dot3(y, mean_mat) * (1.0 / DH_C)
    yc = y - mu
    var = _dot3(yc * yc, mean_mat) * (1.0 / DH_C)
    yn = yc * lax.rsqrt(var + RWKV_GN_EPS) * lnw_ref[...] + lnb_ref[...]
    o_ref[...] = ((yn + bonus_ref[...]) * g_ref[...]).astype(o_ref.dtype)


def _rwkv_post(y, bonus, g, lnw, lnb):
    m = y.shape[0]
    tm = _tile(m, 512)
    row = pl.BlockSpec((tm, C_W), lambda i: (i, 0))
    vec = _const_spec((1, C_W))
    return pl.pallas_call(
        _rwkv_post_kernel, grid=(m // tm,),
        in_specs=[row, row, row, vec, vec], out_specs=row,
        out_shape=jax.ShapeDtypeStruct((m, C_W), BF16),
        compiler_params=_cp("parallel"), name="rwkv_post",
    )(y, bonus, g, lnw, lnb)


def _mla_pre_kernel(z_ref, qn_ref, kvn_ref, wq_ref, bdk_ref, cq_ref, sq_ref, ck_ref, sk_ref,
                    qa_ref, qp_ref, c_ref, kpe_ref):
    z = z_ref[...]
    qn = _rms(z[:, :Q_LORA], qn_ref[...]).astype(BF16)
    qf = _dot(qn, wq_ref[...])
    n_nope = H_D * NOPE_D
    n_pe = H_D * ROPE_D
    q_pe = qf[:, n_nope:n_nope + n_pe] * cq_ref[...] + qf[:, n_nope + n_pe:] * sq_ref[...]
    qa_ref[...] = _dot(qf[:, :n_nope].astype(BF16), bdk_ref[...]).astype(BF16)
    qp_ref[...] = q_pe.astype(BF16)
    c_ref[...] = _rms(z[:, Q_LORA:Q_LORA + KV_LORA], kvn_ref[...])
    o = Q_LORA + KV_LORA
    kpe_ref[...] = z[:, o:o + ROPE_D] * ck_ref[...] + z[:, o + ROPE_D:o + 2 * ROPE_D] * sk_ref[...]


def _mla_pre(z_d, p, cos_q, sin_q, cos_k, sin_k, *, tseq):
    m = z_d.shape[0]
    if tseq < 256:
        cos_q, sin_q, cos_k, sin_k = (jnp.tile(t, (m // tseq, 1)) for t in (cos_q, sin_q, cos_k, sin_k))
        tseq = m
    tm = _tile(min(m, tseq), 512)
    nt = tseq // tm
    row = lambda width: pl.BlockSpec((tm, width), lambda i: (i, 0))
    pos = lambda width: pl.BlockSpec((tm, width), lambda i: (i % nt, 0))
    n_pe = H_D * ROPE_D
    return pl.pallas_call(
        _mla_pre_kernel, grid=(m // tm,),
        in_specs=[row(D_PAD), _const_spec((1, Q_LORA)), _const_spec((1, KV_LORA)), _const_spec(p["wq"].shape),
                  _const_spec(p["bdk"].shape), pos(n_pe), pos(n_pe), pos(ROPE_D), pos(ROPE_D)],
        out_specs=[row(H_D * KV_LORA), row(n_pe), row(KV_LORA), row(ROPE_D)],
        out_shape=[jax.ShapeDtypeStruct((m, H_D * KV_LORA), BF16), jax.ShapeDtypeStruct((m, n_pe), BF16),
                   jax.ShapeDtypeStruct((m, KV_LORA), F32), jax.ShapeDtypeStruct((m, ROPE_D), F32)],
        compiler_params=_cp("parallel"), name="mla_pre",
    )(z_d, p["q_norm"], p["kv_norm"], p["wq"], p["bdk"], cos_q, sin_q, cos_k, sin_k)


def _ffn_kernel(h_ref, p_ref, tail_ref, *rest, tseq, final, n_tails):
    tail_refs = (tail_ref,) + rest[:n_tails - 1]
    (gf_ref, wup_ref, cw_ref, cb_ref, wdn_ref, gp_ref, wpg_ref, bpg_ref, wple_ref, gfin_ref,
     o_ref, buf_ref, carry_ref) = rest[n_tails - 1:]
    tm = h_ref.shape[0]
    taps = CONV_W - 1
    h = h_ref[...]
    xn = _rms(h, gf_ref[...]).astype(BF16)
    up = _dot(xn, wup_ref[...])
    a = up[:, :D_FF]
    b = up[:, D_FF:]
    if tseq >= tm:
        @pl.when((pl.program_id(0) * tm) % tseq == 0)
        def _():
            carry_ref[...] = tail_refs[0][0]
        tails = carry_ref[...]
    else:
        tails = [r[...] for r in tail_refs]
    c = cb_ref[...] + cw_ref[taps:taps + 1, :] * a
    for d in range(1, CONV_W):
        c = c + cw_ref[taps - d:taps - d + 1, :] * _shifted(a, d, tseq, tails)
    if tseq >= tm:
        carry_ref[...] = a[tm - taps:, :]
        buf_ref[0] = a[tm - taps:, :]
    else:
        buf_ref[...] = a.reshape(tm // tseq, tseq, D_FF)[:, tseq - taps:, :]
    f = _dot((_gelu(c) * b).astype(BF16), wdn_ref[...])
    h = h + f
    gate = _sigmoid(_dot(_rms(h, gp_ref[...]).astype(BF16), wpg_ref[...]) + bpg_ref[...])
    h = h + gate * _dot(p_ref[...].astype(BF16), wple_ref[...])
    if final:
        h = _rms(h, gfin_ref[...])
    o_ref[...] = h


def _ffn(h, p, tails, w, *, tseq, final):
    m = h.shape[0]
    n_seq = m // tseq
    taps = CONV_W - 1
    tm = _tile(min(m, tseq) if tseq >= 256 else m, 256)
    row = lambda width: pl.BlockSpec((tm, width), lambda i: (i, 0))
    single = lambda shape: pl.BlockSpec(shape, lambda *_: (0,) * len(shape), pipeline_mode=pl.Buffered(1))
    if tseq >= tm:
        nt = tseq // tm
        tail_specs = [pl.BlockSpec((1, taps, D_FF), lambda i: (i // nt, 0, 0))]
        buf_spec = pl.BlockSpec((1, taps, D_FF), lambda i: (i // nt, 0, 0))
    else:
        assert tseq >= taps
        tail_specs = [row(D_FF)] * taps
        buf_spec = pl.BlockSpec((tm // tseq, taps, D_FF), lambda i: (i, 0, 0))
    vec = _const_spec((1, D_MODEL))
    return pl.pallas_call(
        functools.partial(_ffn_kernel, tseq=tseq, final=final, n_tails=len(tail_specs)),
        grid=(m // tm,),
        in_specs=[row(D_MODEL), row(PLE_DIM)] + tail_specs + [
            vec, single((D_MODEL, 2 * D_FF)), _const_spec((CONV_W, D_FF)), _const_spec((1, D_FF)),
            single((D_FF, D_MODEL)), vec, single((D_MODEL, D_MODEL)), vec, single((PLE_DIM, D_MODEL)), vec],
        out_specs=[row(D_MODEL), buf_spec],
        out_shape=[jax.ShapeDtypeStruct((m, D_MODEL), F32), jax.ShapeDtypeStruct((n_seq, taps, D_FF), F32)],
        scratch_shapes=[pltpu.VMEM((taps, D_FF), F32)],
        compiler_params=_cp("arbitrary"), name="conv_ffn_ple",
    )(h, p, *tails, w["norm_ffn"], w["w_up"], w["conv_w"], w["conv_b"], w["w_down"], w["norm_ple"],
      w["w_pg"], w["b_pg"], w["w_ple"], w["norm_f"])


def _block_diag(blocks):
    n, r, c = blocks.shape
    eye = jnp.eye(n, dtype=blocks.dtype)
    return (eye[:, None, :, None] * blocks[:, :, None, :]).reshape(n * r, n * c)


def _pad_cols(x, width):
    return jnp.pad(x, [(0, 0)] * (x.ndim - 1) + [(0, width - x.shape[-1])])


def _rope_tables(pos, reps):
    half = ROPE_D // 2
    inv = ROPE_THETA ** (-jnp.arange(half, dtype=F32) / half)
    ang = pos.astype(F32)[:, None] * inv[None, :]
    cos = jnp.cos(ang)
    sin = jnp.sin(ang)
    cos = jnp.concatenate([cos, cos], axis=-1)
    sin = jnp.concatenate([-sin, sin], axis=-1)
    return jnp.tile(cos, (1, reps)), jnp.tile(sin, (1, reps))


def _swap_halves_cols(w, group):
    shp = w.shape
    w = w.reshape(shp[:-1] + (shp[-1] // group, 2, group // 2))
    return w[..., ::-1, :].reshape(shp)


def _even_weights(W, j):
    w = W["w_in_e"][j]
    o = 3 * A_W
    w_r = jnp.concatenate([w[:, :o], w[:, o + H_A:], _pad_cols(w[:, o:o + H_A], LANE)], axis=1).astype(BF16)
    ab, bbr, bbi = _s5_prep(
        W["s5_lam_re"][j].reshape(1, S5_W), W["s5_lam_im"][j].reshape(1, S5_W),
        jnp.repeat(W["s5_log_dt"][j], S5_N).reshape(1, S5_W),
        W["s5_b_re"][j].reshape(S5_W, S5_P).T, W["s5_b_im"][j].reshape(S5_W, S5_P).T)
    to_bd = lambda t: _block_diag(t.T.reshape(S5_GROUPS, S5_N, S5_P).transpose(0, 2, 1)).astype(BF16)
    to_cd = lambda c: _block_diag(c.transpose(0, 2, 1)).astype(BF16)
    return dict(
        w_in=w_r, b_f=_pad_cols(W["b_f"][j].reshape(1, H_A), LANE), ab=ab, bd_re=to_bd(bbr), bd_im=to_bd(bbi),
        cd_re=to_cd(W["s5_c_re"][j]), cd_im=to_cd(W["s5_c_im"][j]), d=W["s5_d"][j].reshape(1, B_W),
        w_glu=W["w_glu"][j].astype(BF16), b_glu=W["b_glu"][j].reshape(1, B_W),
        w_out=W["w_out_e"][j].astype(BF16))


def _pad_c(x):
    o = 3 * C_W
    return jnp.concatenate([x[..., :o], _pad_cols(x[..., o:o + W_LORA], LANE),
                            _pad_cols(x[..., o + W_LORA:o + W_LORA + A_LORA], LANE),
                            _pad_cols(x[..., o + W_LORA + A_LORA:], LANE)], axis=-1)


def _unpad_c(x):
    o = 3 * C_W
    return jnp.concatenate([x[..., :o], x[..., o:o + W_LORA], x[..., o + LANE:o + LANE + A_LORA],
                            x[..., o + 2 * LANE:o + 2 * LANE + G_LORA]], axis=-1)


def _pad_rows(x, rows):
    return jnp.pad(x, [(0, rows - x.shape[0]), (0, 0)])


def _odd_weights(W, j):
    w = W["w_in_o"][j]
    wd = w[:, C_IN:]
    o = Q_LORA + KV_LORA
    kpe_w = wd[:, o:o + ROPE_D]
    w_d = _pad_cols(jnp.concatenate([wd[:, :o], kpe_w, _swap_halves_cols(kpe_w, ROPE_D)], axis=1), D_PAD)
    w_r = jnp.concatenate([_pad_c(w[:, :C_IN]), w_d], axis=1).astype(BF16)
    wq = W["w_q_up"][j]
    wq_pe = wq[..., NOPE_D:].reshape(Q_LORA, H_D * ROPE_D)
    wq_r = jnp.concatenate([wq[..., :NOPE_D].reshape(Q_LORA, H_D * NOPE_D), wq_pe,
                            _swap_halves_cols(wq_pe, ROPE_D)], axis=1).astype(BF16)
    wkv = W["w_kv_up"][j]
    bdk = _block_diag(wkv[..., :NOPE_D].transpose(1, 2, 0)).astype(BF16)
    bdv = _block_diag(wkv[..., NOPE_D:].transpose(1, 0, 2)).astype(BF16)
    vec = lambda x: x.reshape(1, -1)
    rw = dict(mu=vec(_pad_c(W["rw_mu"][j])), w0=vec(W["rw_w0"][j]),
              w2=_pad_rows(W["rw_w2"][j], LANE).astype(BF16), a0=vec(W["rw_a0"][j]),
              a2=_pad_rows(W["rw_a2"][j], LANE).astype(BF16), g2=_pad_rows(W["rw_g2"][j], LANE).astype(BF16),
              kk=vec(W["rw_kk"][j]), ka=vec(W["rw_ka"][j]), rk=vec(W["rw_rk"][j]))
    mla = dict(q_norm=vec(W["mla_q_norm"][j]), kv_norm=vec(W["mla_kv_norm"][j]), wq=wq_r, bdk=bdk)
    return dict(w_in=w_r, rw=rw, mla=mla, lnw=vec(W["rw_lnw"][j]), lnb=vec(W["rw_lnb"][j]), bdv=bdv,
                w_out=W["w_out_o"][j].astype(BF16))


def _ffn_weights(W, i):
    vec = lambda x: x.reshape(1, -1)
    return dict(norm_ffn=vec(W["norm_ffn"][i]), w_up=W["w_ffn_up"][i].astype(BF16), conv_w=W["ffn_conv_w"][i],
                conv_b=vec(W["ffn_conv_b"][i]), w_down=W["w_ffn_down"][i].astype(BF16),
                norm_ple=vec(W["norm_ple"][i]), w_pg=W["w_pg"][i].astype(BF16), b_pg=vec(W["b_pg"][i]),
                w_ple=W["w_ple"][i].astype(BF16), norm_f=vec(W["norm_f"]))


def _fox_prompt(q, k, lf, v, n_seq, tseq):
    _, hi, mid, lo = _cumsum(lf.reshape(n_seq, tseq, LANE), tseq)
    heads = lambda x: x.reshape(n_seq, tseq, H_A, DH_A)
    col = lambda x: x[..., :H_A, None]
    one = jnp.ones((n_seq, tseq, H_A, 3), BF16)
    zero = jnp.zeros((n_seq, tseq, H_A, LANE - DH_A - 6), BF16)
    qa = jnp.concatenate([heads(q), col(hi), col(mid), col(lo), one, zero], axis=-1)
    ka = jnp.concatenate([heads(k.astype(BF16)), one, col(-hi), col(-mid), col(-lo), zero], axis=-1)
    to_hm = lambda x: x.transpose(0, 2, 1, 3).reshape(n_seq * H_A, tseq, x.shape[-1])
    o = _flash(to_hm(qa), to_hm(ka), to_hm(heads(v.astype(BF16))), rpt=1, tq=512, tk=512, scale=1.0)
    return o.reshape(n_seq, H_A, tseq, DH_A).transpose(0, 2, 1, 3).reshape(n_seq * tseq, A_W)


def _fox_decode(j, q, k, lf, v, n_seq, tseq, caches, page_table):
    cache_k, cache_v, cache_lf_t = caches
    cn = _cumsum(lf.reshape(1, n_seq * tseq, LANE), tseq)[0].reshape(n_seq, tseq, LANE)[..., :H_A]
    eye = jnp.eye(H_A, dtype=BF16)
    qh = q.reshape(n_seq, tseq, H_A, DH_A)
    q_bd = (qh[:, :, :, None, :] * eye[None, None, :, :, None]).reshape(n_seq, tseq * H_A, A_W)
    cn_rows = jnp.broadcast_to(cn.reshape(n_seq, tseq * H_A, 1), (n_seq, tseq * H_A, LANE))
    cn_keys = _pad_cols(cn.transpose(0, 2, 1), LANE)
    new_t = lambda x: _pad_cols(x.astype(BF16).reshape(n_seq, tseq, A_W).transpose(0, 2, 1), LANE)
    o = _fox_sample(j, page_table, q_bd, new_t(k), new_t(v), cn_rows, cn_keys, cache_k, cache_v, cache_lf_t)
    return o.reshape(n_seq * tseq, A_W)


def _even_layer(h, ew, j, n_seq, tseq, s5_re, s5_im, fox_ctx):
    q, k, v, u, lf = _even_in(h, ew["norm_mix"], ew["w_in"], ew["b_f"])
    if fox_ctx is None:
        o_a = _fox_prompt(q, k, lf, v, n_seq, tseq)
    else:
        o_a = _fox_decode(j, q, k, lf, v, n_seq, tseq, *fox_ctx)
    y_b, n_re, n_im = _s5(u, ew["ab"], ew["bd_re"], ew["bd_im"], ew["cd_re"], ew["cd_im"], ew["d"],
                          ew["w_glu"], ew["b_glu"], s5_re.reshape(n_seq, S5_W), s5_im.reshape(n_seq, S5_W),
                          n_seq=n_seq, tseq=tseq)
    h = _even_out(h, o_a, y_b, ew["w_out"])
    rows = (k.reshape(n_seq, tseq, H_A, DH_A), v.reshape(n_seq, tseq, H_A, DH_A),
            lf[:, :H_A].reshape(n_seq, tseq, H_A))
    st = (n_re.reshape(n_seq, S5_GROUPS, S5_N), n_im.reshape(n_seq, S5_GROUPS, S5_N))
    return h, rows, st


def _odd_layer(h, ow, j, n_seq, tseq, pos, rw_state, rw_shift, mla_ctx):
    m = h.shape[0]
    z_c, z_d = _odd_in(h, ow["norm_mix"], ow["w_in"])
    shift_pad = _pad_c(rw_shift.astype(F32))
    if tseq >= 256:
        tails = shift_pad.reshape(n_seq, 1, C_PAD)
    else:
        tails = jnp.repeat(shift_pad, tseq, axis=0)
    r, w, k, v, kk, kka, g, bonus = _rwkv_pre(z_c, tails, ow["rw"], tseq=tseq)
    y, s_pairs = _rwkv_chunked(r, w, k, v, kk, kka, _state_to_pairs(rw_state.astype(F32)), tseq=tseq)
    s_new = _pairs_to_state(s_pairs)
    o_c = _rwkv_post(y, bonus, g, ow["lnw"], ow["lnb"])
    sh = _unpad_c(z_c.reshape(n_seq, tseq, C_PAD)[:, -1])

    cos_q, sin_q = _rope_tables(pos, H_D)
    cos_k, sin_k = _rope_tables(pos, 1)
    q_abs, q_pe, c, kpe = _mla_pre(z_d, ow["mla"], cos_q, sin_q, cos_k, sin_k, tseq=tseq)
    if mla_ctx is None:
        rows = tseq * H_D
        qq = jnp.concatenate([q_abs.reshape(n_seq, rows, KV_LORA), q_pe.reshape(n_seq, rows, ROPE_D),
                              jnp.zeros((n_seq, rows, QK_PAD - KV_LORA - ROPE_D), BF16)], axis=-1)
        cb = c.astype(BF16).reshape(n_seq, tseq, KV_LORA)
        kq = jnp.concatenate([cb, kpe.astype(BF16).reshape(n_seq, tseq, ROPE_D),
                              jnp.zeros((n_seq, tseq, QK_PAD - KV_LORA - ROPE_D), BF16)], axis=-1)
        lat = _flash(qq, kq, cb, rpt=H_D, tq=128, tk=512, scale=(NOPE_D + ROPE_D) ** -0.5)
    else:
        cache_ckv, cache_kpe, page_table = mla_ctx
        rows = tseq * H_D
        c_new = jnp.pad(c.astype(BF16).reshape(n_seq, tseq, KV_LORA), ((0, 0), (0, LANE - tseq), (0, 0)))
        kpe_new_t = _pad_cols(kpe.astype(BF16).reshape(n_seq, tseq, ROPE_D).transpose(0, 2, 1), LANE)
        lat = _mla_sample(j, page_table, q_abs.reshape(n_seq, rows, KV_LORA), q_pe.reshape(n_seq, rows, ROPE_D),
                          c_new, kpe_new_t, cache_ckv, cache_kpe)
    h = _odd_out(h, o_c, lat.reshape(m, H_D * KV_LORA), ow["bdv"], ow["w_out"])
    return h, (s_new, sh), (c.reshape(n_seq, tseq, KV_LORA), kpe.reshape(n_seq, tseq, ROPE_D))


def _ffn_layer(h, p, buf, fw, n_seq, tseq, final):
    taps = CONV_W - 1
    if tseq >= 256:
        tails = [buf.astype(F32)]
    else:
        tails = [jnp.repeat(buf[:, i].astype(F32), tseq, axis=0) for i in range(taps)]
    return _ffn(h, p, tails, fw, tseq=tseq, final=final)


def _trunk(x, p, pos, s5_re, s5_im, rw_state, rw_shift, ffn_buf, weights, fox_ctx, mla_ctx):
    n_seq, tseq, _ = x.shape
    depth = p.shape[0]
    m = n_seq * tseq
    h = x.reshape(m, D_MODEL)
    fox_rows, s5_states, rw_states, mla_rows, ffn_bufs = [], [], [], [], []
    for i in range(depth):
        j = i // 2
        lw = weights["mix"][i]
        if i % 2 == 0:
            h, rows, st = _even_layer(h, lw, j, n_seq, tseq, s5_re[j], s5_im[j], fox_ctx)
            fox_rows.append(rows)
            s5_states.append(st)
        else:
            h, st, rows = _odd_layer(h, lw, j, n_seq, tseq, pos, rw_state[j], rw_shift[j], mla_ctx)
            rw_states.append(st)
            mla_rows.append(rows)
        h, nb = _ffn_layer(h, p[i].reshape(m, PLE_DIM), ffn_buf[i], weights["ffn"][i], n_seq, tseq,
                           final=(i == depth - 1))
        ffn_bufs.append(nb)
    stk = lambda xs, n: jnp.stack([r[n] for r in xs])
    return (h.reshape(n_seq, tseq, D_MODEL), stk(fox_rows, 0), stk(fox_rows, 1), stk(fox_rows, 2),
            stk(s5_states, 0), stk(s5_states, 1), stk(rw_states, 0), stk(rw_states, 1),
            stk(mla_rows, 0), stk(mla_rows, 1), jnp.stack(ffn_bufs))


def kernel(x_prompt, x_sample, cache_fox_k, cache_fox_v, cache_fox_lf, state_s5_re, state_s5_im, state_rwkv, state_shift, cache_mla_ckv, cache_mla_kpe, state_ffn_conv, page_table, p_prompt, p_sample, norm_mix, w_in_e, b_f, s5_lam_re, s5_lam_im, s5_log_dt, s5_b_re, s5_b_im, s5_c_re, s5_c_im, s5_d, w_glu, b_glu, w_out_e, w_in_o, rw_mu, rw_w0, rw_w2, rw_a0, rw_a2, rw_g2, rw_kk, rw_ka, rw_rk, rw_lnw, rw_lnb, mla_q_norm, w_q_up, mla_kv_norm, w_kv_up, w_out_o, norm_ffn, w_ffn_up, ffn_conv_w, ffn_conv_b, w_ffn_down, norm_ple, w_pg, b_pg, w_ple, norm_f):
    W = dict(w_in_e=w_in_e, b_f=b_f, s5_lam_re=s5_lam_re, s5_lam_im=s5_lam_im, s5_log_dt=s5_log_dt,
             s5_b_re=s5_b_re, s5_b_im=s5_b_im, s5_c_re=s5_c_re, s5_c_im=s5_c_im, s5_d=s5_d, w_glu=w_glu,
             b_glu=b_glu, w_out_e=w_out_e, w_in_o=w_in_o, rw_mu=rw_mu, rw_w0=rw_w0, rw_w2=rw_w2, rw_a0=rw_a0,
             rw_a2=rw_a2, rw_g2=rw_g2, rw_kk=rw_kk, rw_ka=rw_ka, rw_rk=rw_rk, rw_lnw=rw_lnw, rw_lnb=rw_lnb,
             mla_q_norm=mla_q_norm, w_q_up=w_q_up, mla_kv_norm=mla_kv_norm, w_kv_up=w_kv_up, w_out_o=w_out_o,
             norm_ffn=norm_ffn, w_ffn_up=w_ffn_up, ffn_conv_w=ffn_conv_w, ffn_conv_b=ffn_conv_b,
             w_ffn_down=w_ffn_down, norm_ple=norm_ple, w_pg=w_pg, b_pg=b_pg, w_ple=w_ple, norm_f=norm_f)
    depth = p_prompt.shape[0]
    ne = (depth + 1) // 2
    no = depth // 2
    mix = []
    for i in range(depth):
        lw = _even_weights(W, i // 2) if i % 2 == 0 else _odd_weights(W, i // 2)
        lw["norm_mix"] = norm_mix[i].reshape(1, D_MODEL)
        mix.append(lw)
    weights = dict(mix=mix, ffn=[_ffn_weights(W, i) for i in range(depth)])

    bp, sp, _ = x_prompt.shape
    zeros = lambda *shape: jnp.zeros(shape, F32)
    out_p = _trunk(x_prompt, p_prompt, jnp.arange(sp), zeros(ne, bp, S5_GROUPS, S5_N),
                   zeros(ne, bp, S5_GROUPS, S5_N), zeros(no, bp, H_C, DH_C, DH_C), zeros(no, bp, C_IN),
                   zeros(depth, bp, CONV_W - 1, D_FF), weights, None, None)

    n_pool, page = cache_fox_k.shape[1], cache_fox_k.shape[2]
    past = page_table.shape[1] * page
    tok_minor = lambda x: x.transpose(0, 1, 3, 4, 2).reshape(ne, n_pool, A_W, page)
    fox_ctx = ((tok_minor(cache_fox_k), tok_minor(cache_fox_v), cache_fox_lf.astype(F32).transpose(0, 1, 3, 2)),
               page_table)
    mla_ctx = (cache_mla_ckv, cache_mla_kpe.transpose(0, 1, 3, 2), page_table)
    out_s = _trunk(x_sample, p_sample, past + jnp.arange(x_sample.shape[1]), state_s5_re, state_s5_im,
                   state_rwkv, state_shift, state_ffn_conv, weights, fox_ctx, mla_ctx)

    (y_p, fk_p, fv_p, fl_p, sr_p, si_p, rw_p, sh_p, ck_p, kp_p, ff_p) = out_p
    (y_s, fk_s, fv_s, fl_s, sr_s, si_s, rw_s, sh_s, ck_s, kp_s, ff_s) = out_s
    return (y_p, y_s, fk_p, fv_p, fl_p, fk_s, fv_s, fl_s, sr_p, si_p, sr_s, si_s, rw_p, sh_p, rw_s, sh_s,
            ck_p, kp_p, ck_s, kp_s, ff_p, ff_s)
```

```python
import functools
import math

import jax
import jax.numpy as jnp
from jax import lax
from jax.experimental import pallas as pl
from jax.experimental.pallas import tpu as pltpu

F32 = jnp.float32
BF16 = jnp.bfloat16

D_MODEL = 1024
H_A = 8
DH_A = 64
A_W = H_A * DH_A
S5_GROUPS = 32
S5_P = 16
S5_N = 64
B_W = S5_GROUPS * S5_P
S5_W = S5_GROUPS * S5_N
H_C = 8
DH_C = 64
C_W = H_C * DH_C
W_LORA = 32
A_LORA = 32
G_LORA = 96
C_IN = 3 * C_W + W_LORA + A_LORA + G_LORA
RWKV_GN_EPS = 64e-5
H_D = 8
Q_LORA = 256
KV_LORA = 128
NOPE_D = 64
ROPE_D = 32
V_D = 64
ROPE_THETA = 10000.0
D_FF = 2816
CONV_W = 3
PLE_DIM = 256
NORM_EPS = 1e-6
NEG = -1e30

LANE = 128
SUBLANE = 8
VMEM_LIMIT = 56 * 1024 * 1024
C_PAD = 3 * C_W + 3 * LANE
D_PAD = 512
QK_PAD = 256
FOX_PAGES_PER_STEP = 16
MLA_PAGES_PER_STEP = 32
S5_SCAN_LANES = 512
RW_CHUNK = 32
RW_SUPER = 256
RW_PAIRS_PER_STEP = 1
S5_BLOCK_IN = LANE
S5_BLOCK_ST = LANE // S5_P * S5_N


def _cp(*sem):
    return pltpu.CompilerParams(dimension_semantics=sem, vmem_limit_bytes=VMEM_LIMIT)


def _tile(n, pref):
    if n <= pref:
        return n
    t = pref - pref % SUBLANE
    while t >= SUBLANE:
        if n % t == 0:
            return t
        t -= SUBLANE
    return n


def _const_spec(shape):
    nd = len(shape)
    return pl.BlockSpec(shape, lambda *_: (0,) * nd)


def _rms(x, g):
    return x * lax.rsqrt(jnp.mean(x * x, axis=-1, keepdims=True) + NORM_EPS) * g


def _split3(x):
    hi = x.astype(BF16)
    r1 = x - hi.astype(F32)
    mid = r1.astype(BF16)
    lo = (r1 - mid.astype(F32)).astype(BF16)
    return hi, mid, lo


def _dot(a, b):
    return jnp.dot(a, b, preferred_element_type=F32)


def _dot3(x, m):
    hi, mid, lo = _split3(x)
    return _dot(hi, m) + _dot(mid, m) + _dot(lo, m)


def _dot3_left(m, x):
    hi, mid, lo = _split3(x)
    return _dot(m, hi) + _dot(m, mid) + _dot(m, lo)


def _log_sigmoid(x):
    return jnp.minimum(x, 0.0) - jnp.log1p(jnp.exp(-jnp.abs(x)))


def _sigmoid(x):
    return 1.0 / (1.0 + jnp.exp(-x))


def _softplus(x):
    return jnp.maximum(x, 0.0) + jnp.log1p(jnp.exp(-jnp.abs(x)))


def _gelu(x):
    return jax.nn.gelu(x, approximate=True)


def _head_ones(width, head):
    r = lax.broadcasted_iota(jnp.int32, (width, width), 0) // head
    c = lax.broadcasted_iota(jnp.int32, (width, width), 1) // head
    return (r == c).astype(BF16)


def _even_in_kernel(h_ref, g_ref, w_ref, bf_ref, q_ref, k_ref, v_ref, u_ref, lf_ref):
    xn = _rms(h_ref[...], g_ref[...]).astype(BF16)
    z = _dot(xn, w_ref[...])
    q_ref[...] = (z[:, :A_W] * (DH_A ** -0.5)).astype(BF16)
    k_ref[...] = z[:, A_W:2 * A_W]
    v_ref[...] = z[:, 2 * A_W:3 * A_W]
    u_ref[...] = z[:, 3 * A_W:3 * A_W + B_W]
    lf_ref[...] = _log_sigmoid(z[:, 3 * A_W + B_W:] + bf_ref[...])


def _even_in(h, g, w, bf):
    m = h.shape[0]
    tm = _tile(m, 512)
    n = w.shape[1]
    row = lambda width: pl.BlockSpec((tm, width), lambda i: (i, 0))
    return pl.pallas_call(
        _even_in_kernel,
        grid=(m // tm,),
        in_specs=[row(D_MODEL), _const_spec((1, D_MODEL)), _const_spec((D_MODEL, n)), _const_spec((1, LANE))],
        out_specs=[row(A_W), row(A_W), row(A_W), row(B_W), row(LANE)],
        out_shape=[jax.ShapeDtypeStruct((m, A_W), BF16), jax.ShapeDtypeStruct((m, A_W), F32),
                   jax.ShapeDtypeStruct((m, A_W), F32), jax.ShapeDtypeStruct((m, B_W), F32),
                   jax.ShapeDtypeStruct((m, LANE), F32)],
        compiler_params=_cp("parallel"),
        name="even_in",
    )(h, g, w, bf)


def _cumsum_kernel(x_ref, c_ref, hi_ref, mid_ref, lo_ref, carry_ref, *, tseq):
    tc = x_ref.shape[1]
    ri = lax.broadcasted_iota(jnp.int32, (tc, tc), 0)
    ci = lax.broadcasted_iota(jnp.int32, (tc, tc), 1)
    if tseq >= tc:
        tri = (ri >= ci).astype(BF16)

        @pl.when((pl.program_id(1) * tc) % tseq == 0)
        def _():
            carry_ref[...] = jnp.zeros_like(carry_ref)
    else:
        tri = ((ri >= ci) & (ri // tseq == ci // tseq)).astype(BF16)
    c = _dot3_left(tri, x_ref[0])
    if tseq >= tc:
        c = c + carry_ref[...]
        carry_ref[...] = c[tc - 1:tc, :]
    c_ref[0] = c
    hi, mid, lo = _split3(c)
    hi_ref[0] = hi
    mid_ref[0] = mid
    lo_ref[0] = lo


def _cumsum(x, tseq):
    g, length, _ = x.shape
    tc = _tile(length, 512)
    assert tseq % tc == 0 or tc % tseq == 0
    blk = pl.BlockSpec((1, tc, LANE), lambda i, j: (i, j, 0))
    return pl.pallas_call(
        functools.partial(_cumsum_kernel, tseq=tseq),
        grid=(g, length // tc),
        in_specs=[blk],
        out_specs=[blk] * 4,
        out_shape=[jax.ShapeDtypeStruct(x.shape, F32)] + [jax.ShapeDtypeStruct(x.shape, BF16)] * 3,
        scratch_shapes=[pltpu.VMEM((1, LANE), F32)],
        compiler_params=_cp("arbitrary", "arbitrary"),
        name="fox_cumsum",
    )(x)


def _flash_kernel(q_ref, k_ref, v_ref, o_ref, *, rpt, tk, scale, dv):
    rows = q_ref.shape[1]
    tq = rows // rpt
    t0 = pl.program_id(1) * tq
    q = q_ref[0]
    if scale == 1.0:
        factor, ex = 1.0, jnp.exp
    else:
        factor, ex = scale * math.log2(math.e), jnp.exp2

    ones_col = v_ref.shape[2] > dv

    def scores(kc):
        k0 = pl.multiple_of(kc * tk, tk)
        return lax.dot_general(q, k_ref[0, pl.ds(k0, tk), :], _NT, preferred_element_type=F32)

    def softmax_pv(kc, s, carry, masked):
        m, l, acc = carry
        k0 = pl.multiple_of(kc * tk, tk)
        vb = v_ref[0, pl.ds(k0, tk), :]
        if masked:
            tok = t0 + lax.broadcasted_iota(jnp.int32, (rows, tk), 0) // rpt
            key = k0 + lax.broadcasted_iota(jnp.int32, (rows, tk), 1)
            s = jnp.where(key <= tok, s, NEG)
        if factor != 1.0:
            s = s * factor
        m_new = jnp.maximum(m, jnp.max(s, axis=1, keepdims=True))
        p = ex(s - m_new)
        alpha = ex(m - m_new)
        if not ones_col:
            l = alpha * l + jnp.sum(p, axis=1, keepdims=True)
        acc = alpha * acc + _dot(p.astype(BF16), vb)
        return m_new, l, acc

    assert tq <= tk
    n_full = t0 // tk

    def pair(i, carry):
        ss = [scores(4 * i + u) for u in range(4)]
        for u in range(4):
            carry = softmax_pv(4 * i + u, ss[u], carry, False)
        return carry

    init = (jnp.full((rows, 1), NEG, F32), jnp.zeros((rows, 1), F32), jnp.zeros((rows, v_ref.shape[2]), F32))
    carry = lax.fori_loop(0, n_full // 4, pair, init)
    carry = lax.fori_loop((n_full // 4) * 4, n_full, lambda kc, c: softmax_pv(kc, scores(kc), c, False), carry)
    _, l, acc = softmax_pv(n_full, scores(n_full), carry, True)
    if ones_col:
        l = acc[:, dv:dv + 1]
    o_ref[0] = (acc[:, :dv] / l).astype(o_ref.dtype)


def _flash(q, k, v, *, rpt, tq, tk, scale):
    g, rows_total, dq = q.shape
    s = k.shape[1]
    dv = v.shape[2]
    tq = min(tq, s)
    tk = min(tk, s)
    assert s % tq == 0 and s % tk == 0 and (tk % tq == 0 or tq % tk == 0)
    rows = tq * rpt
    if dv % LANE:
        dva = dv + LANE - dv % LANE
        v_aug = jnp.concatenate([v, jnp.ones((g, s, 1), BF16), jnp.zeros((g, s, dva - dv - 1), BF16)], axis=-1)
    else:
        dva, v_aug = dv, v
    return pl.pallas_call(
        functools.partial(_flash_kernel, rpt=rpt, tk=tk, scale=scale, dv=dv),
        grid=(g, s // tq),
        in_specs=[pl.BlockSpec((1, rows, dq), lambda i, j: (i, j, 0)),
                  pl.BlockSpec((1, s, dq), lambda i, j: (i, 0, 0)),
                  pl.BlockSpec((1, s, dva), lambda i, j: (i, 0, 0))],
        out_specs=pl.BlockSpec((1, rows, dv), lambda i, j: (i, j, 0)),
        out_shape=jax.ShapeDtypeStruct((g, rows_total, dv), BF16),
        compiler_params=_cp("parallel", "arbitrary"),
        name="flash_prompt",
    )(q, k, v_aug)


_NT = (((1,), (1,)), ((), ()))


def _online_update(s, vals, m_ref, l_ref, acc_ref, *, vals_transposed=False):
    m_old = m_ref[...]
    m_new = m_old
    for blk in s:
        m_new = jnp.maximum(m_new, jnp.max(blk, axis=1, keepdims=True))
    alpha = jnp.exp(m_old - m_new)
    l = alpha * l_ref[...]
    acc = alpha * acc_ref[...]
    for blk, val in zip(s, vals):
        p = jnp.exp(blk - m_new)
        l = l + jnp.sum(p, axis=1, keepdims=True)
        if vals_transposed:
            acc = acc + lax.dot_general(p.astype(BF16), val, _NT, preferred_element_type=F32)
        else:
            acc = acc + _dot(p.astype(BF16), val)
    m_ref[...] = m_new
    l_ref[...] = l
    acc_ref[...] = acc


def _new_key_mask(rows, t_new):
    tok = lax.broadcasted_iota(jnp.int32, (rows, LANE), 0) // (rows // t_new)
    key = lax.broadcasted_iota(jnp.int32, (rows, LANE), 1)
    return key <= tok


def _fox_sample_kernel(pt_ref, q_ref, kn_ref, vn_ref, cnr_ref, cnk_ref, *rest, npg, t_new):
    k_refs = rest[:npg]
    v_refs = rest[npg:2 * npg]
    lf_refs = rest[2 * npg:3 * npg]
    o_ref, m_ref, l_ref, acc_ref, carry_ref = rest[3 * npg:]
    c = pl.program_id(1)
    rows = q_ref.shape[1]
    q = q_ref[0]
    cn_rows = cnr_ref[0]

    @pl.when(c == 0)
    def _():
        m_ref[...] = jnp.full_like(m_ref, NEG)
        l_ref[...] = jnp.zeros_like(l_ref)
        acc_ref[...] = jnp.zeros_like(acc_ref)
        carry_ref[...] = jnp.zeros_like(carry_ref)
        s = _dot(q, kn_ref[0])
        s = s + cn_rows - jnp.tile(cnk_ref[0], (t_new, 1))
        s = jnp.where(_new_key_mask(rows, t_new), s, NEG)
        _online_update([s], [vn_ref[0]], m_ref, l_ref, acc_ref, vals_transposed=True)

    ri = lax.broadcasted_iota(jnp.int32, (LANE, LANE), 0)
    ci = lax.broadcasted_iota(jnp.int32, (LANE, LANE), 1)
    after = (ri > ci).astype(BF16)
    ones = jnp.ones((LANE, LANE), BF16)
    lf_all = jnp.concatenate([r[...] for r in lf_refs], axis=0)
    suffix = _dot3(lf_all, after)
    total = _dot3(lf_all, ones)
    carry = carry_ref[...]
    scores = [None] * npg
    for i in reversed(range(npg)):
        s = _dot(q, k_refs[i][...].astype(BF16))
        bias = suffix[i * H_A:(i + 1) * H_A] + carry
        scores[i] = s + cn_rows + jnp.tile(bias, (t_new, 1))
        carry = carry + total[i * H_A:(i + 1) * H_A]
    carry_ref[...] = carry
    _online_update(scores, [r[...].astype(BF16) for r in v_refs], m_ref, l_ref, acc_ref, vals_transposed=True)

    @pl.when(c == pl.num_programs(1) - 1)
    def _():
        out = acc_ref[...] / l_ref[...]
        rh = lax.broadcasted_iota(jnp.int32, out.shape, 0) % H_A
        ch = lax.broadcasted_iota(jnp.int32, out.shape, 1) // DH_A
        out = jnp.where(rh == ch, out, 0.0).reshape(t_new, H_A, A_W)
        o_ref[0] = jnp.sum(out, axis=1).astype(o_ref.dtype)


def _fox_sample(layer, page_table, q_bd, k_new_t, v_new_t, cn_rows, cn_keys, cache_k_t, cache_v_t, cache_lf_t):
    b, n_pages = page_table.shape
    page = cache_k_t.shape[3]
    npg = min(FOX_PAGES_PER_STEP, n_pages)
    assert page == LANE and n_pages % npg == 0
    n_chunks = n_pages // npg
    rows = q_bd.shape[1]
    t_new = rows // H_A

    def seq_spec(shape):
        return pl.BlockSpec((1,) + shape, lambda i, c, pt: (i, 0, 0))

    def page_spec(shape, idx):
        def imap(i, c, pt):
            return (layer, pt[i, (n_chunks - 1 - c) * npg + idx], 0, 0)
        return pl.BlockSpec((None, None) + shape, imap)

    in_specs = [seq_spec((rows, A_W)), seq_spec((A_W, LANE)), seq_spec((A_W, LANE)),
                seq_spec((rows, LANE)), seq_spec((H_A, LANE))]
    in_specs += [page_spec((A_W, page), i) for i in range(npg)]
    in_specs += [page_spec((A_W, page), i) for i in range(npg)]
    in_specs += [page_spec((H_A, page), i) for i in range(npg)]
    grid_spec = pltpu.PrefetchScalarGridSpec(
        num_scalar_prefetch=1, grid=(b, n_chunks), in_specs=in_specs,
        out_specs=pl.BlockSpec((1, t_new, A_W), lambda i, c, pt: (i, 0, 0)),
        scratch_shapes=[pltpu.VMEM((rows, 1), F32), pltpu.VMEM((rows, 1), F32),
                        pltpu.VMEM((rows, A_W), F32), pltpu.VMEM((H_A, LANE), F32)])
    return pl.pallas_call(
        functools.partial(_fox_sample_kernel, npg=npg, t_new=t_new),
        grid_spec=grid_spec,
        out_shape=jax.ShapeDtypeStruct((b, t_new, A_W), BF16),
        compiler_params=_cp("parallel", "arbitrary"),
        name="fox_sample",
    )(page_table, q_bd, k_new_t, v_new_t, cn_rows, cn_keys,
      *([cache_k_t] * npg), *([cache_v_t] * npg), *([cache_lf_t] * npg))


def _mla_sample_kernel(pt_ref, qa_ref, qp_ref, cn_ref, kn_ref, *rest, npg, t_new, scale):
    c_refs = rest[:npg]
    p_refs = rest[npg:2 * npg]
    o_ref, m_ref, l_ref, acc_ref = rest[2 * npg:]
    c = pl.program_id(1)
    rows = qa_ref.shape[1]
    qa = qa_ref[0]
    qp = qp_ref[0]

    def score(cb, pb_t):
        return (lax.dot_general(qa, cb, _NT, preferred_element_type=F32) + _dot(qp, pb_t)) * scale

    @pl.when(c == 0)
    def _():
        m_ref[...] = jnp.full_like(m_ref, NEG)
        l_ref[...] = jnp.zeros_like(l_ref)
        acc_ref[...] = jnp.zeros_like(acc_ref)
        s = jnp.where(_new_key_mask(rows, t_new), score(cn_ref[0], kn_ref[0]), NEG)
        _online_update([s], [cn_ref[0]], m_ref, l_ref, acc_ref)

    cbs = [r[...].astype(BF16) for r in c_refs]
    scores = [score(cb, r[...].astype(BF16)) for cb, r in zip(cbs, p_refs)]
    _online_update(scores, cbs, m_ref, l_ref, acc_ref)

    @pl.when(c == pl.num_programs(1) - 1)
    def _():
        o_ref[0] = (acc_ref[...] / l_ref[...]).astype(o_ref.dtype)


def _mla_sample(layer, page_table, q_abs, q_pe, c_new, kpe_new_t, cache_ckv, cache_kpe_t):
    b, n_pages = page_table.shape
    page = cache_ckv.shape[2]
    npg = min(MLA_PAGES_PER_STEP, n_pages)
    assert page == LANE and n_pages % npg == 0
    n_chunks = n_pages // npg
    rows = q_abs.shape[1]
    t_new = rows // H_D

    def seq_spec(shape):
        return pl.BlockSpec((1,) + shape, lambda i, c, pt: (i, 0, 0))

    def page_spec(shape, idx):
        def imap(i, c, pt):
            return (layer, pt[i, c * npg + idx], 0, 0)
        return pl.BlockSpec((None, None) + shape, imap)

    in_specs = [seq_spec((rows, KV_LORA)), seq_spec((rows, ROPE_D)),
                seq_spec((LANE, KV_LORA)), seq_spec((ROPE_D, LANE))]
    in_specs += [page_spec((page, KV_LORA), i) for i in range(npg)]
    in_specs += [page_spec((ROPE_D, page), i) for i in range(npg)]
    grid_spec = pltpu.PrefetchScalarGridSpec(
        num_scalar_prefetch=1, grid=(b, n_chunks), in_specs=in_specs,
        out_specs=pl.BlockSpec((1, rows, KV_LORA), lambda i, c, pt: (i, 0, 0)),
        scratch_shapes=[pltpu.VMEM((rows, 1), F32), pltpu.VMEM((rows, 1), F32),
                        pltpu.VMEM((rows, KV_LORA), F32)])
    return pl.pallas_call(
        functools.partial(_mla_sample_kernel, npg=npg, t_new=t_new, scale=(NOPE_D + ROPE_D) ** -0.5),
        grid_spec=grid_spec,
        out_shape=jax.ShapeDtypeStruct((b, rows, KV_LORA), BF16),
        compiler_params=_cp("parallel", "arbitrary"),
        name="mla_sample",
    )(page_table, q_abs, q_pe, c_new, kpe_new_t, *([cache_ckv] * npg), *([cache_kpe_t] * npg))


def _s5_prep_kernel(lr_ref, li_ref, ldt_ref, br_ref, bi_ref, ab_ref, bbr_ref, bbi_ref):
    lr = lr_ref[...]
    li = li_ref[...]
    dt = jnp.exp(ldt_ref[...])
    mag = jnp.exp(lr * dt)
    ang = li * dt
    ab_re = mag * jnp.cos(ang)
    ab_im = mag * jnp.sin(ang)
    den = lr * lr + li * li
    f_re = ((ab_re - 1.0) * lr + ab_im * li) / den
    f_im = (ab_im * lr - (ab_re - 1.0) * li) / den
    ab_ref[0:1, :] = ab_re
    ab_ref[1:2, :] = ab_im
    br = br_ref[...]
    bi = bi_ref[...]
    bbr_ref[...] = f_re * br - f_im * bi
    bbi_ref[...] = f_re * bi + f_im * br


def _s5_prep(lam_re, lam_im, log_dt, b_re_t, b_im_t):
    return pl.pallas_call(
        _s5_prep_kernel,
        out_shape=[jax.ShapeDtypeStruct((2, S5_W), F32), jax.ShapeDtypeStruct((S5_P, S5_W), F32),
                   jax.ShapeDtypeStruct((S5_P, S5_W), F32)],
        name="s5_prep",
    )(lam_re, lam_im, log_dt, b_re_t, b_im_t)


def _cmul(ar, ai, br, bi):
    return ar * br - ai * bi, ar * bi + ai * br


def _s5_kernel(u_ref, ab_ref, bdr_ref, bdi_ref, cdr_ref, cdi_ref, d_ref, wg_ref, bg_ref, h0r_ref, h0i_ref,
               y_ref, sr_ref, si_ref, xr_ref, xi_ref, car_ref, cai_ref, *, grouped):
    tc = u_ref.shape[0]
    u = u_ref[...]
    ub = u.astype(BF16)
    n_blk = B_W // S5_BLOCK_IN
    for j in range(n_blk):
        ui = slice(j * S5_BLOCK_IN, (j + 1) * S5_BLOCK_IN)
        xs = slice(j * S5_BLOCK_ST, (j + 1) * S5_BLOCK_ST)
        xr_ref[:, xs] = _dot(ub[:, ui], bdr_ref[ui, xs])
        xi_ref[:, xs] = _dot(ub[:, ui], bdi_ref[ui, xs])

    if not grouped:
        @pl.when(pl.program_id(1) == 0)
        def _():
            car_ref[...] = h0r_ref[0]
            cai_ref[...] = h0i_ref[0]

    cw = S5_SCAN_LANES
    tile = (SUBLANE, cw)
    row = lax.broadcasted_iota(jnp.int32, tile, 0)
    n_groups = tc // SUBLANE
    for j in range(S5_W // cw):
        lanes = slice(j * cw, (j + 1) * cw)
        a1r = jnp.broadcast_to(ab_ref[0:1, lanes], tile)
        a1i = jnp.broadcast_to(ab_ref[1:2, lanes], tile)
        a2r, a2i = _cmul(a1r, a1i, a1r, a1i)
        a4r, a4i = _cmul(a2r, a2i, a2r, a2i)
        apr, api = a1r, a1i
        for d, (pr, pi) in ((1, (a1r, a1i)), (2, (a2r, a2i)), (4, (a4r, a4i))):
            sr = pltpu.roll(apr, d, axis=0)
            si = pltpu.roll(api, d, axis=0)
            mr, mi = _cmul(pr, pi, sr, si)
            apr = jnp.where(row >= d, mr, apr)
            api = jnp.where(row >= d, mi, api)

        def body(g, carry):
            cr, ci = carry
            r0 = pl.multiple_of(g * SUBLANE, SUBLANE)
            xr = xr_ref[pl.ds(r0, SUBLANE), lanes]
            xi = xi_ref[pl.ds(r0, SUBLANE), lanes]
            for d, (pr, pi) in ((1, (a1r, a1i)), (2, (a2r, a2i)), (4, (a4r, a4i))):
                sr = jnp.where(row >= d, pltpu.roll(xr, d, axis=0), 0.0)
                si = jnp.where(row >= d, pltpu.roll(xi, d, axis=0), 0.0)
                mr, mi = _cmul(pr, pi, sr, si)
                xr = xr + mr
                xi = xi + mi
            if grouped:
                cr = jnp.broadcast_to(h0r_ref[pl.ds(g, 1), lanes], tile)
                ci = jnp.broadcast_to(h0i_ref[pl.ds(g, 1), lanes], tile)
            mr, mi = _cmul(apr, api, cr, ci)
            xr = xr + mr
            xi = xi + mi
            xr_ref[pl.ds(r0, SUBLANE), lanes] = xr
            xi_ref[pl.ds(r0, SUBLANE), lanes] = xi
            if grouped:
                sr_ref[pl.ds(g, 1), lanes] = xr[SUBLANE - 1:SUBLANE]
                si_ref[pl.ds(g, 1), lanes] = xi[SUBLANE - 1:SUBLANE]
                return cr, ci
            last_r = jnp.broadcast_to(xr[SUBLANE - 1:SUBLANE], tile)
            last_i = jnp.broadcast_to(xi[SUBLANE - 1:SUBLANE], tile)
            return last_r, last_i

        if grouped:
            init = (jnp.zeros(tile, F32), jnp.zeros(tile, F32))
        else:
            init = (jnp.broadcast_to(car_ref[0:1, lanes], tile), jnp.broadcast_to(cai_ref[0:1, lanes], tile))
        cr, ci = lax.fori_loop(0, n_groups, body, init)
        if not grouped:
            car_ref[0:1, lanes] = cr[0:1]
            cai_ref[0:1, lanes] = ci[0:1]
            sr_ref[0, 0:1, lanes] = cr[0:1]
            si_ref[0, 0:1, lanes] = ci[0:1]

    ys = []
    for j in range(n_blk):
        ui = slice(j * S5_BLOCK_IN, (j + 1) * S5_BLOCK_IN)
        xs = slice(j * S5_BLOCK_ST, (j + 1) * S5_BLOCK_ST)
        ys.append(_dot(xr_ref[:, xs].astype(BF16), cdr_ref[xs, ui])
                  - _dot(xi_ref[:, xs].astype(BF16), cdi_ref[xs, ui]))
    y = jnp.concatenate(ys, axis=1) + d_ref[...] * u
    y = _gelu(y)
    y = y * _sigmoid(_dot(y.astype(BF16), wg_ref[...]) + bg_ref[...])
    y_ref[...] = y.astype(y_ref.dtype)


def _s5(u, ab, bd_re, bd_im, cd_re, cd_im, d, w_glu, b_glu, h0_re, h0_im, *, n_seq, tseq):
    m = u.shape[0]
    grouped = tseq == SUBLANE
    consts = [_const_spec((2, S5_W)), _const_spec((B_W, S5_W)), _const_spec((B_W, S5_W)),
              _const_spec((S5_W, B_W)), _const_spec((S5_W, B_W)), _const_spec((1, B_W)),
              _const_spec((B_W, B_W)), _const_spec((1, B_W))]
    if grouped:
        tc = _tile(m, 256)
        ng = tc // SUBLANE
        grid = (m // tc, 1)
        u_spec = pl.BlockSpec((tc, B_W), lambda i, j: (i, 0))
        st_spec = pl.BlockSpec((ng, S5_W), lambda i, j: (i, 0))
        h0_spec = st_spec
        st_shape = jax.ShapeDtypeStruct((n_seq, S5_W), F32)
    else:
        tc = _tile(tseq, 256)
        assert tc % SUBLANE == 0
        nt = tseq // tc
        grid = (n_seq, nt)
        u_spec = pl.BlockSpec((tc, B_W), lambda i, j: (i * nt + j, 0))
        st_spec = pl.BlockSpec((1, 1, S5_W), lambda i, j: (i, 0, 0))
        h0_spec = st_spec
        st_shape = jax.ShapeDtypeStruct((n_seq, 1, S5_W), F32)
        h0_re = h0_re.reshape(n_seq, 1, S5_W)
        h0_im = h0_im.reshape(n_seq, 1, S5_W)
    y, s_re, s_im = pl.pallas_call(
        functools.partial(_s5_kernel, grouped=grouped),
        grid=grid,
        in_specs=[u_spec] + consts + [h0_spec, h0_spec],
        out_specs=[u_spec, st_spec, st_spec],
        out_shape=[jax.ShapeDtypeStruct((m, B_W), BF16), st_shape, st_shape],
        scratch_shapes=[pltpu.VMEM((tc, S5_W), F32), pltpu.VMEM((tc, S5_W), F32),
                        pltpu.VMEM((1, S5_W), F32), pltpu.VMEM((1, S5_W), F32)],
        compiler_params=_cp("arbitrary", "arbitrary"),
        name="s5_mix",
    )(u, ab, bd_re, bd_im, cd_re, cd_im, d, w_glu, b_glu, h0_re, h0_im)
    return y, s_re.reshape(n_seq, S5_W), s_im.reshape(n_seq, S5_W)


def _even_out_kernel(h_ref, a_ref, b_ref, w_ref, o_ref):
    o_ref[...] = h_ref[...] + _dot(a_ref[...], w_ref[0:A_W, :]) + _dot(b_ref[...], w_ref[A_W:, :])


def _odd_out_kernel(h_ref, a_ref, lat_ref, bdv_ref, w_ref, o_ref):
    o_d = _dot(lat_ref[...], bdv_ref[...]).astype(BF16)
    o_ref[...] = h_ref[...] + _dot(a_ref[...], w_ref[0:C_W, :]) + _dot(o_d, w_ref[C_W:, :])


def _even_out(h, a, b, w):
    m = h.shape[0]
    tm = _tile(m, 512)
    row = lambda width: pl.BlockSpec((tm, width), lambda i: (i, 0))
    return pl.pallas_call(
        _even_out_kernel, grid=(m // tm,),
        in_specs=[row(D_MODEL), row(A_W), row(B_W), _const_spec(w.shape)],
        out_specs=row(D_MODEL), out_shape=jax.ShapeDtypeStruct((m, D_MODEL), F32),
        compiler_params=_cp("parallel"), name="even_out",
    )(h, a, b, w)


def _odd_out(h, a, lat, bdv, w):
    m = h.shape[0]
    tm = _tile(m, 512)
    row = lambda width: pl.BlockSpec((tm, width), lambda i: (i, 0))
    return pl.pallas_call(
        _odd_out_kernel, grid=(m // tm,),
        in_specs=[row(D_MODEL), row(C_W), row(H_D * KV_LORA), _const_spec(bdv.shape), _const_spec(w.shape)],
        out_specs=row(D_MODEL), out_shape=jax.ShapeDtypeStruct((m, D_MODEL), F32),
        compiler_params=_cp("parallel"), name="odd_out",
    )(h, a, lat, bdv, w)


def _shifted(x, d, tseq, tails):
    tm = x.shape[0]
    rolled = pltpu.roll(x, d, axis=0)
    row = lax.broadcasted_iota(jnp.int32, (tm, 1), 0)
    if tseq >= tm:
        taps = tails.shape[0]
        out = rolled
        for i in range(d):
            out = jnp.where(row == i, tails[taps - d + i:taps - d + i + 1, :], out)
        return out
    taps = len(tails)
    pos = row % tseq
    out = rolled
    for i in range(d):
        out = jnp.where(pos == i, tails[taps - d + i], out)
    return out


def _odd_in_kernel(h_ref, g_ref, w_ref, zc_ref, zd_ref):
    xn = _rms(h_ref[...], g_ref[...]).astype(BF16)
    z = _dot(xn, w_ref[...])
    zc_ref[...] = z[:, :C_PAD]
    zd_ref[...] = z[:, C_PAD:]


def _odd_in(h, g, w):
    m = h.shape[0]
    tm = _tile(m, 512)
    row = lambda width: pl.BlockSpec((tm, width), lambda i: (i, 0))
    return pl.pallas_call(
        _odd_in_kernel, grid=(m // tm,),
        in_specs=[row(D_MODEL), _const_spec((1, D_MODEL)), _const_spec(w.shape)],
        out_specs=[row(C_PAD), row(D_PAD)],
        out_shape=[jax.ShapeDtypeStruct((m, C_PAD), F32), jax.ShapeDtypeStruct((m, D_PAD), F32)],
        compiler_params=_cp("parallel"), name="odd_in",
    )(h, g, w)


def _rwkv_pre_kernel(z_ref, tail_ref, mu_ref, w0_ref, w2_ref, a0_ref, a2_ref, g2_ref, kkw_ref, kaw_ref,
                     rk_ref, r_ref, w_ref, k_ref, v_ref, kk_ref, kka_ref, g_ref, bonus_ref, carry_ref,
                     *, tseq):
    tm = z_ref.shape[0]
    z = z_ref[...]
    if tseq >= tm:
        @pl.when((pl.program_id(0) * tm) % tseq == 0)
        def _():
            carry_ref[...] = tail_ref[0]
        zprev = _shifted(z, 1, tseq, carry_ref[...])
        carry_ref[...] = z[tm - 1:tm, :]
    else:
        zprev = _shifted(z, 1, tseq, [tail_ref[...]])
    zm = z + (zprev - z) * mu_ref[...]
    r = zm[:, :C_W]
    k = zm[:, C_W:2 * C_W]
    v = zm[:, 2 * C_W:3 * C_W]
    wd = zm[:, 3 * C_W:3 * C_W + LANE]
    ad = zm[:, 3 * C_W + LANE:3 * C_W + 2 * LANE]
    gd = zm[:, 3 * C_W + 2 * LANE:]
    w_log = -_softplus(-(w0_ref[...] + _dot(jnp.tanh(wd).astype(BF16), w2_ref[...]))) - 0.5
    log_decay = -jnp.exp(w_log)
    a = _sigmoid(a0_ref[...] + _dot(ad.astype(BF16), a2_ref[...]))
    g = _dot(_sigmoid(gd).astype(BF16), g2_ref[...])
    ones = _head_ones(C_W, DH_C)
    kk = k * kkw_ref[...]
    norm = jnp.sqrt(_dot3(kk * kk, ones))
    kk = kk / jnp.maximum(norm, 1e-12)
    k2 = k * (1.0 + (a - 1.0) * kaw_ref[...])
    r_ref[...] = r
    w_ref[...] = log_decay
    k_ref[...] = k2
    v_ref[...] = v
    kk_ref[...] = kk
    kka_ref[...] = kk * a
    g_ref[...] = g
    bonus_ref[...] = _dot3(r * k2 * rk_ref[...], ones) * v


def _rwkv_pre(z_c, tails, p, *, tseq):
    m = z_c.shape[0]
    tm = _tile(min(m, tseq) if tseq >= 256 else m, 256)
    row = lambda width: pl.BlockSpec((tm, width), lambda i: (i, 0))
    if tseq >= tm:
        nt = tseq // tm
        tail_spec = pl.BlockSpec((1, 1, C_PAD), lambda i: (i // nt, 0, 0))
    else:
        tail_spec = row(C_PAD)
    vec = _const_spec((1, C_W))
    return pl.pallas_call(
        functools.partial(_rwkv_pre_kernel, tseq=tseq), grid=(m // tm,),
        in_specs=[row(C_PAD), tail_spec, _const_spec((1, C_PAD)), vec, _const_spec((LANE, C_W)), vec,
                  _const_spec((LANE, C_W)), _const_spec((LANE, C_W)), vec, vec, vec],
        out_specs=[row(C_W)] * 8,
        out_shape=[jax.ShapeDtypeStruct((m, C_W), F32)] * 8,
        scratch_shapes=[pltpu.VMEM((1, C_PAD), F32)],
        compiler_params=_cp("arbitrary"), name="rwkv_pre",
    )(z_c, tails, p["mu"], p["w0"], p["w2"], p["a0"], p["a2"], p["g2"], p["kk"], p["ka"], p["rk"])


def _split2(x):
    hi = x.astype(BF16)
    return hi, (x - hi.astype(F32)).astype(BF16)


def _mm3(xh, xl, yh, yl):
    return _dot(xh, yh) + _dot(xh, yl) + _dot(xl, yh)


def _rwkv_chunk_kernel(r_ref, lw_ref, k_ref, v_ref, kk_ref, kka_ref, s0_ref, y_ref, so_ref, s_ref,
                       *, chunk, carry):
    n = r_ref.shape[0]
    n_chunks = n // chunk
    if carry:
        @pl.when(pl.program_id(2) == 0)
        def _():
            s_ref[...] = s0_ref[0]

    ri = lax.broadcasted_iota(jnp.int32, (n, n), 0)
    ci = lax.broadcasted_iota(jnp.int32, (n, n), 1)
    same = (ri // chunk) == (ci // chunk)
    incl = same & (ci <= ri)
    strict = same & (ci < ri)
    incl_b = incl.astype(BF16)
    same_b = same.astype(BF16)
    eye_n = (ri == ci).astype(F32)
    lane_head = lax.broadcasted_iota(jnp.int32, (n, LANE), 1) // DH_C
    ri2 = lax.broadcasted_iota(jnp.int32, (LANE, LANE), 0)
    ci2 = lax.broadcasted_iota(jnp.int32, (LANE, LANE), 1)
    same_head = (ri2 // DH_C) == (ci2 // DH_C)
    diag = ri2 == ci2
    col_chunk = lax.broadcasted_iota(jnp.int32, (LANE, n), 1) // chunk
    stack = lambda x: jnp.concatenate([jnp.where(col_chunk == c, x, 0.0) for c in range(n_chunks)],
                                      axis=0).astype(BF16)

    for pp in range(r_ref.shape[1] // LANE):
        lanes = slice(pp * LANE, (pp + 1) * LANE)
        r, lw, k, v, kk, b = (x[:, lanes] for x in (r_ref, lw_ref, k_ref, v_ref, kk_ref, kka_ref))
        cl = _dot3_left(incl_b, lw)
        tot = _dot3_left(same_b, lw)
        e_inv = jnp.exp(-cl)
        e_end = jnp.exp(tot - cl)
        alpha = -kk * jnp.exp(cl - lw)
        rho = r * jnp.exp(cl)
        beta_end = b * e_end
        kappa_end = k * e_end
        lam_end = jnp.exp(tot)
        vb = v.astype(BF16)
        rhs = jnp.concatenate([b * e_inv, k * e_inv], axis=0).astype(BF16)

        a_eff = jnp.zeros((n, LANE), F32)
        u0 = jnp.zeros((n, LANE), F32)
        p_eff = rho
        y0 = jnp.zeros((n, LANE), F32)
        for hh in range(LANE // DH_C):
            hmask = lane_head == hh
            a_h = jnp.where(hmask, alpha, 0.0)
            r_h = jnp.where(hmask, rho, 0.0)
            lhs = jnp.concatenate([a_h, r_h], axis=0).astype(BF16)
            g = lax.dot_general(lhs, rhs, _NT, preferred_element_type=F32)
            nmat = jnp.where(strict, g[:n, :n], 0.0)
            mmat = jnp.where(strict, g[:n, n:], 0.0).astype(BF16)
            pb = jnp.where(incl, g[n:, :n], 0.0).astype(BF16)
            pk = jnp.where(incl, g[n:, n:], 0.0).astype(BF16)
            ph = nmat.astype(BF16)
            tmat = eye_n + nmat
            for _ in range(int(math.log2(chunk)) - 1):
                ph = _dot(ph, ph).astype(BF16)
                tmat = tmat + _dot(tmat.astype(BF16), ph)
            th, tl = _split2(tmat)
            zin = jnp.concatenate([a_h, _dot(mmat, vb)], axis=1).astype(BF16)
            z = _dot(th, zin) + _dot(tl, zin)
            a_eff = a_eff + z[:, :LANE]
            u0 = jnp.where(hmask, z[:, LANE:], u0)
            w = _dot(pb, z.astype(BF16))
            p_eff = p_eff + w[:, :LANE]
            y0 = jnp.where(hmask, w[:, LANE:] + _dot(pk, vb), y0)

        a_b = a_eff.astype(BF16)
        u0_b = u0.astype(BF16)
        bt_all = stack(beta_end.T)
        g_all = _dot(bt_all, a_b)
        e_all = _dot(bt_all, u0_b) + _dot(stack(kappa_end.T), vb)
        if carry:
            s = s_ref[pp]
        for c in range(n_chunks):
            rows = slice(c * chunk, (c + 1) * chunk)
            blk = slice(c * LANE, (c + 1) * LANE)
            if not carry:
                s = s0_ref[c, pp]
            lam = jnp.broadcast_to(lam_end[c * chunk:c * chunk + 1, :], (LANE, LANE))
            dmat = jnp.where(same_head, g_all[blk], 0.0) + jnp.where(diag, lam, 0.0)
            emat = jnp.where(same_head, e_all[blk], 0.0)
            sh, sl = _split2(s)
            y_ref[rows, lanes] = _dot(p_eff[rows].astype(BF16), sh) + y0[rows]
            dh, dl = _split2(dmat)
            s = _mm3(dh, dl, sh, sl) + emat
            if not carry:
                so_ref[c, pp] = s
        if carry:
            s_ref[pp] = s
            so_ref[0, pp] = s


def _rwkv_chunked(r, lw, k, v, kk, kka, s0_pairs, *, tseq):
    m = r.shape[0]
    n_seq = m // tseq
    pps = RW_PAIRS_PER_STEP
    n_groups = C_W // (LANE * pps)
    carry = tseq >= RW_SUPER
    if carry:
        n, chunk = RW_SUPER, RW_CHUNK
        assert tseq % n == 0
        nt = tseq // n
        grid = (n_seq, n_groups, nt)
        row = pl.BlockSpec((n, pps * LANE), lambda i, p, j: (i * nt + j, p))
        st = pl.BlockSpec((1, pps, LANE, LANE), lambda i, p, j: (i, p, 0, 0))
    else:
        chunk = tseq
        n = _tile(m, RW_SUPER)
        assert chunk & (chunk - 1) == 0 and chunk % SUBLANE == 0 and n % chunk == 0
        grid = (m // n, n_groups, 1)
        row = pl.BlockSpec((n, pps * LANE), lambda i, p, j: (i, p))
        st = pl.BlockSpec((n // chunk, pps, LANE, LANE), lambda i, p, j: (i, p, 0, 0))
    return pl.pallas_call(
        functools.partial(_rwkv_chunk_kernel, chunk=chunk, carry=carry), grid=grid,
        in_specs=[row] * 6 + [st],
        out_specs=[row, st],
        out_shape=[jax.ShapeDtypeStruct((m, C_W), F32), jax.ShapeDtypeStruct(s0_pairs.shape, F32)],
        scratch_shapes=[pltpu.VMEM((pps, LANE, LANE), F32)],
        compiler_params=_cp("arbitrary", "arbitrary", "arbitrary"), name="rwkv_chunked",
    )(r, lw, k, v, kk, kka, s0_pairs)


def _state_to_pairs(s):
    n = s.shape[0]
    per = LANE // DH_C
    st = s.transpose(0, 1, 3, 2).reshape(n, H_C // per, per, DH_C, DH_C)
    eye = jnp.eye(per, dtype=s.dtype)
    bd = st[:, :, :, :, None, :] * eye[None, None, :, None, :, None]
    return bd.reshape(n, H_C // per, LANE, LANE)


def _pairs_to_state(sp):
    n = sp.shape[0]
    per = LANE // DH_C
    x = sp.reshape(n, H_C // per, per, DH_C, per, DH_C)
    d = jnp.stack([x[:, :, i, :, i, :] for i in range(per)], axis=2)
    return d.reshape(n, H_C, DH_C, DH_C).transpose(0, 1, 3, 2)


def _rwkv_post_kernel(y_ref, bonus_ref, g_ref, lnw_ref, lnb_ref, o_ref):
    y = y_ref[...]
    mean_mat = _head_ones(C_W, DH_C)
    mu = _dot3(y, mean_mat) * (1.0 / DH_C)
    yc = y - mu
    var = _dot3(yc * yc, mean_mat) * (1.0 / DH_C)
    yn = yc * lax.rsqrt(var + RWKV_GN_EPS) * lnw_ref[...] + lnb_ref[...]
    o_ref[...] = ((yn + bonus_ref[...]) * g_ref[...]).astype(o_ref.dtype)


def _rwkv_post(y, bonus, g, lnw, lnb):
    m = y.shape[0]
    tm = _tile(m, 512)
    row = pl.BlockSpec((tm, C_W), lambda i: (i, 0))
    vec = _const_spec((1, C_W))
    return pl.pallas_call(
        _rwkv_post_kernel, grid=(m // tm,),
        in_specs=[row, row, row, vec, vec], out_specs=row,
        out_shape=jax.ShapeDtypeStruct((m, C_W), BF16),
        compiler_params=_cp("parallel"), name="rwkv_post",
    )(y, bonus, g, lnw, lnb)


def _mla_pre_kernel(z_ref, qn_ref, kvn_ref, wq_ref, bdk_ref, cq_ref, sq_ref, ck_ref, sk_ref,
                    qa_ref, qp_ref, c_ref, kpe_ref):
    z = z_ref[...]
    qn = _rms(z[:, :Q_LORA], qn_ref[...]).astype(BF16)
    qf = _dot(qn, wq_ref[...])
    n_nope = H_D * NOPE_D
    n_pe = H_D * ROPE_D
    q_pe = qf[:, n_nope:n_nope + n_pe] * cq_ref[...] + qf[:, n_nope + n_pe:] * sq_ref[...]
    qa_ref[...] = _dot(qf[:, :n_nope].astype(BF16), bdk_ref[...]).astype(BF16)
    qp_ref[...] = q_pe.astype(BF16)
    c_ref[...] = _rms(z[:, Q_LORA:Q_LORA + KV_LORA], kvn_ref[...])
    o = Q_LORA + KV_LORA
    kpe_ref[...] = z[:, o:o + ROPE_D] * ck_ref[...] + z[:, o + ROPE_D:o + 2 * ROPE_D] * sk_ref[...]


def _mla_pre(z_d, p, cos_q, sin_q, cos_k, sin_k, *, tseq):
    m = z_d.shape[0]
    if tseq < 256:
        cos_q, sin_q, cos_k, sin_k = (jnp.tile(t, (m // tseq, 1)) for t in (cos_q, sin_q, cos_k, sin_k))
        tseq = m
    tm = _tile(min(m, tseq), 512)
    nt = tseq // tm
    row = lambda width: pl.BlockSpec((tm, width), lambda i: (i, 0))
    pos = lambda width: pl.BlockSpec((tm, width), lambda i: (i % nt, 0))
    n_pe = H_D * ROPE_D
    return pl.pallas_call(
        _mla_pre_kernel, grid=(m // tm,),
        in_specs=[row(D_PAD), _const_spec((1, Q_LORA)), _const_spec((1, KV_LORA)), _const_spec(p["wq"].shape),
                  _const_spec(p["bdk"].shape), pos(n_pe), pos(n_pe), pos(ROPE_D), pos(ROPE_D)],
        out_specs=[row(H_D * KV_LORA), row(n_pe), row(KV_LORA), row(ROPE_D)],
        out_shape=[jax.ShapeDtypeStruct((m, H_D * KV_LORA), BF16), jax.ShapeDtypeStruct((m, n_pe), BF16),
                   jax.ShapeDtypeStruct((m, KV_LORA), F32), jax.ShapeDtypeStruct((m, ROPE_D), F32)],
        compiler_params=_cp("parallel"), name="mla_pre",
    )(z_d, p["q_norm"], p["kv_norm"], p["wq"], p["bdk"], cos_q, sin_q, cos_k, sin_k)


def _ffn_kernel(h_ref, p_ref, tail_ref, *rest, tseq, final, n_tails):
    tail_refs = (tail_ref,) + rest[:n_tails - 1]
    (gf_ref, wup_ref, cw_ref, cb_ref, wdn_ref, gp_ref, wpg_ref, bpg_ref, wple_ref, gfin_ref,
     o_ref, buf_ref, carry_ref) = rest[n_tails - 1:]
    tm = h_ref.shape[0]
    taps = CONV_W - 1
    h = h_ref[...]
    xn = _rms(h, gf_ref[...]).astype(BF16)
    up = _dot(xn, wup_ref[...])
    a = up[:, :D_FF]
    b = up[:, D_FF:]
    if tseq >= tm:
        @pl.when((pl.program_id(0) * tm) % tseq == 0)
        def _():
            carry_ref[...] = tail_refs[0][0]
        tails = carry_ref[...]
    else:
        tails = [r[...] for r in tail_refs]
    c = cb_ref[...] + cw_ref[taps:taps + 1, :] * a
    for d in range(1, CONV_W):
        c = c + cw_ref[taps - d:taps - d + 1, :] * _shifted(a, d, tseq, tails)
    if tseq >= tm:
        carry_ref[...] = a[tm - taps:, :]
        buf_ref[0] = a[tm - taps:, :]
    else:
        buf_ref[...] = a.reshape(tm // tseq, tseq, D_FF)[:, tseq - taps:, :]
    f = _dot((_gelu(c) * b).astype(BF16), wdn_ref[...])
    h = h + f
    gate = _sigmoid(_dot(_rms(h, gp_ref[...]).astype(BF16), wpg_ref[...]) + bpg_ref[...])
    h = h + gate * _dot(p_ref[...].astype(BF16), wple_ref[...])
    if final:
        h = _rms(h, gfin_ref[...])
    o_ref[...] = h


def _ffn(h, p, tails, w, *, tseq, final):
    m = h.shape[0]
    n_seq = m // tseq
    taps = CONV_W - 1
    tm = _tile(min(m, tseq) if tseq >= 256 else m, 256)
    row = lambda width: pl.BlockSpec((tm, width), lambda i: (i, 0))
    single = lambda shape: pl.BlockSpec(shape, lambda *_: (0,) * len(shape), pipeline_mode=pl.Buffered(1))
    if tseq >= tm:
        nt = tseq // tm
        tail_specs = [pl.BlockSpec((1, taps, D_FF), lambda i: (i // nt, 0, 0))]
        buf_spec = pl.BlockSpec((1, taps, D_FF), lambda i: (i // nt, 0, 0))
    else:
        assert tseq >= taps
        tail_specs = [row(D_FF)] * taps
        buf_spec = pl.BlockSpec((tm // tseq, taps, D_FF), lambda i: (i, 0, 0))
    vec = _const_spec((1, D_MODEL))
    return pl.pallas_call(
        functools.partial(_ffn_kernel, tseq=tseq, final=final, n_tails=len(tail_specs)),
        grid=(m // tm,),
        in_specs=[row(D_MODEL), row(PLE_DIM)] + tail_specs + [
            vec, single((D_MODEL, 2 * D_FF)), _const_spec((CONV_W, D_FF)), _const_spec((1, D_FF)),
            single((D_FF, D_MODEL)), vec, single((D_MODEL, D_MODEL)), vec, single((PLE_DIM, D_MODEL)), vec],
        out_specs=[row(D_MODEL), buf_spec],
        out_shape=[jax.ShapeDtypeStruct((m, D_MODEL), F32), jax.ShapeDtypeStruct((n_seq, taps, D_FF), F32)],
        scratch_shapes=[pltpu.VMEM((taps, D_FF), F32)],
        compiler_params=_cp("arbitrary"), name="conv_ffn_ple",
    )(h, p, *tails, w["norm_ffn"], w["w_up"], w["conv_w"], w["conv_b"], w["w_down"], w["norm_ple"],
      w["w_pg"], w["b_pg"], w["w_ple"], w["norm_f"])


def _block_diag(blocks):
    n, r, c = blocks.shape
    eye = jnp.eye(n, dtype=blocks.dtype)
    return (eye[:, None, :, None] * blocks[:, :, None, :]).reshape(n * r, n * c)


def _pad_cols(x, width):
    return jnp.pad(x, [(0, 0)] * (x.ndim - 1) + [(0, width - x.shape[-1])])


def _rope_tables(pos, reps):
    half = ROPE_D // 2
    inv = ROPE_THETA ** (-jnp.arange(half, dtype=F32) / half)
    ang = pos.astype(F32)[:, None] * inv[None, :]
    cos = jnp.cos(ang)
    sin = jnp.sin(ang)
    cos = jnp.concatenate([cos, cos], axis=-1)
    sin = jnp.concatenate([-sin, sin], axis=-1)
    return jnp.tile(cos, (1, reps)), jnp.tile(sin, (1, reps))


def _swap_halves_cols(w, group):
    shp = w.shape
    w = w.reshape(shp[:-1] + (shp[-1] // group, 2, group // 2))
    return w[..., ::-1, :].reshape(shp)


def _even_weights(W, j):
    w = W["w_in_e"][j]
    o = 3 * A_W
    w_r = jnp.concatenate([w[:, :o], w[:, o + H_A:], _pad_cols(w[:, o:o + H_A], LANE)], axis=1).astype(BF16)
    ab, bbr, bbi = _s5_prep(
        W["s5_lam_re"][j].reshape(1, S5_W), W["s5_lam_im"][j].reshape(1, S5_W),
        jnp.repeat(W["s5_log_dt"][j], S5_N).reshape(1, S5_W),
        W["s5_b_re"][j].reshape(S5_W, S5_P).T, W["s5_b_im"][j].reshape(S5_W, S5_P).T)
    to_bd = lambda t: _block_diag(t.T.reshape(S5_GROUPS, S5_N, S5_P).transpose(0, 2, 1)).astype(BF16)
    to_cd = lambda c: _block_diag(c.transpose(0, 2, 1)).astype(BF16)
    return dict(
        w_in=w_r, b_f=_pad_cols(W["b_f"][j].reshape(1, H_A), LANE), ab=ab, bd_re=to_bd(bbr), bd_im=to_bd(bbi),
        cd_re=to_cd(W["s5_c_re"][j]), cd_im=to_cd(W["s5_c_im"][j]), d=W["s5_d"][j].reshape(1, B_W),
        w_glu=W["w_glu"][j].astype(BF16), b_glu=W["b_glu"][j].reshape(1, B_W),
        w_out=W["w_out_e"][j].astype(BF16))


def _pad_c(x):
    o = 3 * C_W
    return jnp.concatenate([x[..., :o], _pad_cols(x[..., o:o + W_LORA], LANE),
                            _pad_cols(x[..., o + W_LORA:o + W_LORA + A_LORA], LANE),
                            _pad_cols(x[..., o + W_LORA + A_LORA:], LANE)], axis=-1)


def _unpad_c(x):
    o = 3 * C_W
    return jnp.concatenate([x[..., :o], x[..., o:o + W_LORA], x[..., o + LANE:o + LANE + A_LORA],
                            x[..., o + 2 * LANE:o + 2 * LANE + G_LORA]], axis=-1)


def _pad_rows(x, rows):
    return jnp.pad(x, [(0, rows - x.shape[0]), (0, 0)])


def _odd_weights(W, j):
    w = W["w_in_o"][j]
    wd = w[:, C_IN:]
    o = Q_LORA + KV_LORA
    kpe_w = wd[:, o:o + ROPE_D]
    w_d = _pad_cols(jnp.concatenate([wd[:, :o], kpe_w, _swap_halves_cols(kpe_w, ROPE_D)], axis=1), D_PAD)
    w_r = jnp.concatenate([_pad_c(w[:, :C_IN]), w_d], axis=1).astype(BF16)
    wq = W["w_q_up"][j]
    wq_pe = wq[..., NOPE_D:].reshape(Q_LORA, H_D * ROPE_D)
    wq_r = jnp.concatenate([wq[..., :NOPE_D].reshape(Q_LORA, H_D * NOPE_D), wq_pe,
                            _swap_halves_cols(wq_pe, ROPE_D)], axis=1).astype(BF16)
    wkv = W["w_kv_up"][j]
    bdk = _block_diag(wkv[..., :NOPE_D].transpose(1, 2, 0)).astype(BF16)
    bdv = _block_diag(wkv[..., NOPE_D:].transpose(1, 0, 2)).astype(BF16)
    vec = lambda x: x.reshape(1, -1)
    rw = dict(mu=vec(_pad_c(W["rw_mu"][j])), w0=vec(W["rw_w0"][j]),
              w2=_pad_rows(W["rw_w2"][j], LANE).astype(BF16), a0=vec(W["rw_a0"][j]),
              a2=_pad_rows(W["rw_a2"][j], LANE).astype(BF16), g2=_pad_rows(W["rw_g2"][j], LANE).astype(BF16),
              kk=vec(W["rw_kk"][j]), ka=vec(W["rw_ka"][j]), rk=vec(W["rw_rk"][j]))
    mla = dict(q_norm=vec(W["mla_q_norm"][j]), kv_norm=vec(W["mla_kv_norm"][j]), wq=wq_r, bdk=bdk)
    return dict(w_in=w_r, rw=rw, mla=mla, lnw=vec(W["rw_lnw"][j]), lnb=vec(W["rw_lnb"][j]), bdv=bdv,
                w_out=W["w_out_o"][j].astype(BF16))


def _ffn_weights(W, i):
    vec = lambda x: x.reshape(1, -1)
    return dict(norm_ffn=vec(W["norm_ffn"][i]), w_up=W["w_ffn_up"][i].astype(BF16), conv_w=W["ffn_conv_w"][i],
                conv_b=vec(W["ffn_conv_b"][i]), w_down=W["w_ffn_down"][i].astype(BF16),
                norm_ple=vec(W["norm_ple"][i]), w_pg=W["w_pg"][i].astype(BF16), b_pg=vec(W["b_pg"][i]),
                w_ple=W["w_ple"][i].astype(BF16), norm_f=vec(W["norm_f"]))


def _fox_prompt(q, k, lf, v, n_seq, tseq):
    _, hi, mid, lo = _cumsum(lf.reshape(n_seq, tseq, LANE), tseq)
    heads = lambda x: x.reshape(n_seq, tseq, H_A, DH_A)
    col = lambda x: x[..., :H_A, None]
    one = jnp.ones((n_seq, tseq, H_A, 3), BF16)
    zero = jnp.zeros((n_seq, tseq, H_A, LANE - DH_A - 6), BF16)
    qa = jnp.concatenate([heads(q), col(hi), col(mid), col(lo), one, zero], axis=-1)
    ka = jnp.concatenate([heads(k.astype(BF16)), one, col(-hi), col(-mid), col(-lo), zero], axis=-1)
    to_hm = lambda x: x.transpose(0, 2, 1, 3).reshape(n_seq * H_A, tseq, x.shape[-1])
    o = _flash(to_hm(qa), to_hm(ka), to_hm(heads(v.astype(BF16))), rpt=1, tq=512, tk=512, scale=1.0)
    return o.reshape(n_seq, H_A, tseq, DH_A).transpose(0, 2, 1, 3).reshape(n_seq * tseq, A_W)


def _fox_decode(j, q, k, lf, v, n_seq, tseq, caches, page_table):
    cache_k, cache_v, cache_lf_t = caches
    cn = _cumsum(lf.reshape(1, n_seq * tseq, LANE), tseq)[0].reshape(n_seq, tseq, LANE)[..., :H_A]
    eye = jnp.eye(H_A, dtype=BF16)
    qh = q.reshape(n_seq, tseq, H_A, DH_A)
    q_bd = (qh[:, :, :, None, :] * eye[None, None, :, :, None]).reshape(n_seq, tseq * H_A, A_W)
    cn_rows = jnp.broadcast_to(cn.reshape(n_seq, tseq * H_A, 1), (n_seq, tseq * H_A, LANE))
    cn_keys = _pad_cols(cn.transpose(0, 2, 1), LANE)
    new_t = lambda x: _pad_cols(x.astype(BF16).reshape(n_seq, tseq, A_W).transpose(0, 2, 1), LANE)
    o = _fox_sample(j, page_table, q_bd, new_t(k), new_t(v), cn_rows, cn_keys, cache_k, cache_v, cache_lf_t)
    return o.reshape(n_seq * tseq, A_W)


def _even_layer(h, ew, j, n_seq, tseq, s5_re, s5_im, fox_ctx):
    q, k, v, u, lf = _even_in(h, ew["norm_mix"], ew["w_in"], ew["b_f"])
    if fox_ctx is None:
        o_a = _fox_prompt(q, k, lf, v, n_seq, tseq)
    else:
        o_a = _fox_decode(j, q, k, lf, v, n_seq, tseq, *fox_ctx)
    y_b, n_re, n_im = _s5(u, ew["ab"], ew["bd_re"], ew["bd_im"], ew["cd_re"], ew["cd_im"], ew["d"],
                          ew["w_glu"], ew["b_glu"], s5_re.reshape(n_seq, S5_W), s5_im.reshape(n_seq, S5_W),
                          n_seq=n_seq, tseq=tseq)
    h = _even_out(h, o_a, y_b, ew["w_out"])
    rows = (k.reshape(n_seq, tseq, H_A, DH_A), v.reshape(n_seq, tseq, H_A, DH_A),
            lf[:, :H_A].reshape(n_seq, tseq, H_A))
    st = (n_re.reshape(n_seq, S5_GROUPS, S5_N), n_im.reshape(n_seq, S5_GROUPS, S5_N))
    return h, rows, st


def _odd_layer(h, ow, j, n_seq, tseq, pos, rw_state, rw_shift, mla_ctx):
    m = h.shape[0]
    z_c, z_d = _odd_in(h, ow["norm_mix"], ow["w_in"])
    shift_pad = _pad_c(rw_shift.astype(F32))
    if tseq >= 256:
        tails = shift_pad.reshape(n_seq, 1, C_PAD)
    else:
        tails = jnp.repeat(shift_pad, tseq, axis=0)
    r, w, k, v, kk, kka, g, bonus = _rwkv_pre(z_c, tails, ow["rw"], tseq=tseq)
    y, s_pairs = _rwkv_chunked(r, w, k, v, kk, kka, _state_to_pairs(rw_state.astype(F32)), tseq=tseq)
    s_new = _pairs_to_state(s_pairs)
    o_c = _rwkv_post(y, bonus, g, ow["lnw"], ow["lnb"])
    sh = _unpad_c(z_c.reshape(n_seq, tseq, C_PAD)[:, -1])

    cos_q, sin_q = _rope_tables(pos, H_D)
    cos_k, sin_k = _rope_tables(pos, 1)
    q_abs, q_pe, c, kpe = _mla_pre(z_d, ow["mla"], cos_q, sin_q, cos_k, sin_k, tseq=tseq)
    if mla_ctx is None:
        rows = tseq * H_D
        qq = jnp.concatenate([q_abs.reshape(n_seq, rows, KV_LORA), q_pe.reshape(n_seq, rows, ROPE_D),
                              jnp.zeros((n_seq, rows, QK_PAD - KV_LORA - ROPE_D), BF16)], axis=-1)
        cb = c.astype(BF16).reshape(n_seq, tseq, KV_LORA)
        kq = jnp.concatenate([cb, kpe.astype(BF16).reshape(n_seq, tseq, ROPE_D),
                              jnp.zeros((n_seq, tseq, QK_PAD - KV_LORA - ROPE_D), BF16)], axis=-1)
        lat = _flash(qq, kq, cb, rpt=H_D, tq=128, tk=512, scale=(NOPE_D + ROPE_D) ** -0.5)
    else:
        cache_ckv, cache_kpe, page_table = mla_ctx
        rows = tseq * H_D
        c_new = jnp.pad(c.astype(BF16).reshape(n_seq, tseq, KV_LORA), ((0, 0), (0, LANE - tseq), (0, 0)))
        kpe_new_t = _pad_cols(kpe.astype(BF16).reshape(n_seq, tseq, ROPE_D).transpose(0, 2, 1), LANE)
        lat = _mla_sample(j, page_table, q_abs.reshape(n_seq, rows, KV_LORA), q_pe.reshape(n_seq, rows, ROPE_D),
                          c_new, kpe_new_t, cache_ckv, cache_kpe)
    h = _odd_out(h, o_c, lat.reshape(m, H_D * KV_LORA), ow["bdv"], ow["w_out"])
    return h, (s_new, sh), (c.reshape(n_seq, tseq, KV_LORA), kpe.reshape(n_seq, tseq, ROPE_D))


def _ffn_layer(h, p, buf, fw, n_seq, tseq, final):
    taps = CONV_W - 1
    if tseq >= 256:
        tails = [buf.astype(F32)]
    else:
        tails = [jnp.repeat(buf[:, i].astype(F32), tseq, axis=0) for i in range(taps)]
    return _ffn(h, p, tails, fw, tseq=tseq, final=final)


def _trunk(x, p, pos, s5_re, s5_im, rw_state, rw_shift, ffn_buf, weights, fox_ctx, mla_ctx):
    n_seq, tseq, _ = x.shape
    depth = p.shape[0]
    m = n_seq * tseq
    h = x.reshape(m, D_MODEL)
    fox_rows, s5_states, rw_states, mla_rows, ffn_bufs = [], [], [], [], []
    for i in range(depth):
        j = i // 2
        lw = weights["mix"][i]
        if i % 2 == 0:
            h, rows, st = _even_layer(h, lw, j, n_seq, tseq, s5_re[j], s5_im[j], fox_ctx)
            fox_rows.append(rows)
            s5_states.append(st)
        else:
            h, st, rows = _odd_layer(h, lw, j, n_seq, tseq, pos, rw_state[j], rw_shift[j], mla_ctx)
            rw_states.append(st)
            mla_rows.append(rows)
        h, nb = _ffn_layer(h, p[i].reshape(m, PLE_DIM), ffn_buf[i], weights["ffn"][i], n_seq, tseq,
                           final=(i == depth - 1))
        ffn_bufs.append(nb)
    stk = lambda xs, n: jnp.stack([r[n] for r in xs])
    return (h.reshape(n_seq, tseq, D_MODEL), stk(fox_rows, 0), stk(fox_rows, 1), stk(fox_rows, 2),
            stk(s5_states, 0), stk(s5_states, 1), stk(rw_states, 0), stk(rw_states, 1),
            stk(mla_rows, 0), stk(mla_rows, 1), jnp.stack(ffn_bufs))


def kernel(x_prompt, x_sample, cache_fox_k, cache_fox_v, cache_fox_lf, state_s5_re, state_s5_im, state_rwkv, state_shift, cache_mla_ckv, cache_mla_kpe, state_ffn_conv, page_table, p_prompt, p_sample, norm_mix, w_in_e, b_f, s5_lam_re, s5_lam_im, s5_log_dt, s5_b_re, s5_b_im, s5_c_re, s5_c_im, s5_d, w_glu, b_glu, w_out_e, w_in_o, rw_mu, rw_w0, rw_w2, rw_a0, rw_a2, rw_g2, rw_kk, rw_ka, rw_rk, rw_lnw, rw_lnb, mla_q_norm, w_q_up, mla_kv_norm, w_kv_up, w_out_o, norm_ffn, w_ffn_up, ffn_conv_w, ffn_conv_b, w_ffn_down, norm_ple, w_pg, b_pg, w_ple, norm_f):
    W = dict(w_in_e=w_in_e, b_f=b_f, s5_lam_re=s5_lam_re, s5_lam_im=s5_lam_im, s5_log_dt=s5_log_dt,
             s5_b_re=s5_b_re, s5_b_im=s5_b_im, s5_c_re=s5_c_re, s5_c_im=s5_c_im, s5_d=s5_d, w_glu=w_glu,
             b_glu=b_glu, w_out_e=w_out_e, w_in_o=w_in_o, rw_mu=rw_mu, rw_w0=rw_w0, rw_w2=rw_w2, rw_a0=rw_a0,
             rw_a2=rw_a2, rw_g2=rw_g2, rw_kk=rw_kk, rw_ka=rw_ka, rw_rk=rw_rk, rw_lnw=rw_lnw, rw_lnb=rw_lnb,
             mla_q_norm=mla_q_norm, w_q_up=w_q_up, mla_kv_norm=mla_kv_norm, w_kv_up=w_kv_up, w_out_o=w_out_o,
             norm_ffn=norm_ffn, w_ffn_up=w_ffn_up, ffn_conv_w=ffn_conv_w, ffn_conv_b=ffn_conv_b,
             w_ffn_down=w_ffn_down, norm_ple=norm_ple, w_pg=w_pg, b_pg=b_pg, w_ple=w_ple, norm_f=norm_f)
    depth = p_prompt.shape[0]
    ne = (depth + 1) // 2
    no = depth // 2
    mix = []
    for i in range(depth):
        lw = _even_weights(W, i // 2) if i % 2 == 0 else _odd_weights(W, i // 2)
        lw["norm_mix"] = norm_mix[i].reshape(1, D_MODEL)
        mix.append(lw)
    weights = dict(mix=mix, ffn=[_ffn_weights(W, i) for i in range(depth)])

    bp, sp, _ = x_prompt.shape
    zeros = lambda *shape: jnp.zeros(shape, F32)
    out_p = _trunk(x_prompt, p_prompt, jnp.arange(sp), zeros(ne, bp, S5_GROUPS, S5_N),
                   zeros(ne, bp, S5_GROUPS, S5_N), zeros(no, bp, H_C, DH_C, DH_C), zeros(no, bp, C_IN),
                   zeros(depth, bp, CONV_W - 1, D_FF), weights, None, None)

    n_pool, page = cache_fox_k.shape[1], cache_fox_k.shape[2]
    past = page_table.shape[1] * page
    tok_minor = lambda x: x.transpose(0, 1, 3, 4, 2).reshape(ne, n_pool, A_W, page)
    fox_ctx = ((tok_minor(cache_fox_k), tok_minor(cache_fox_v), cache_fox_lf.astype(F32).transpose(0, 1, 3, 2)),
               page_table)
    mla_ctx = (cache_mla_ckv, cache_mla_kpe.transpose(0, 1, 3, 2), page_table)
    out_s = _trunk(x_sample, p_sample, past + jnp.arange(x_sample.shape[1]), state_s5_re, state_s5_im,
                   state_rwkv, state_shift, state_ffn_conv, weights, fox_ctx, mla_ctx)

    (y_p, fk_p, fv_p, fl_p, sr_p, si_p, rw_p, sh_p, ck_p, kp_p, ff_p) = out_p
    (y_s, fk_s, fv_s, fl_s, sr_s, si_s, rw_s, sh_s, ck_s, kp_s, ff_s) = out_s
    return (y_p, y_s, fk_p, fv_p, fl_p, fk_s, fv_s, fl_s, sr_p, si_p, sr_s, si_s, rw_p, sh_p, rw_s, sh_s,
            ck_p, kp_p, ck_s, kp_s, ff_p, ff_s)
```
